```python
import math
import jax, jax.numpy as jnp
from jax import lax
import numpy as np

D_MODEL = 1024
BATCH = 32
SEQ = 256
DEPTH = 4
DEC_BATCH = 8
DEC_SEQ = 4096
PAST_LEN = 256

GRID_W = 64
EPS = 1e-6
POOL_WIDTH = 256
POOL_GROUPS = 4
POOL_GC = POOL_WIDTH // POOL_GROUPS
POOL_WINDOWS = (2, 4, 8, 16)
HY_WIDTH = 256
HY_SHORT = 3
HY_BANDS = 8
HY_EMB = 1 + 2 * HY_BANDS
HY_FFN = 64
N_HEADS = 8
Q_RANK = 384
KV_RANK = 256
NOPE_DIM = 64
ROPE_DIM = 32
V_DIM = 64
QK_DIM = NOPE_DIM + ROPE_DIM
ROPE_BASE = 10000.0
Q_BLOCK = 128
N_BRANCH = 3
IN_COLS = POOL_WIDTH + 3 * HY_WIDTH + Q_RANK + KV_RANK + ROPE_DIM + N_BRANCH * D_MODEL
FF_DIM = 2816
N_EXPERTS = 8
TOP_K = 2
EXPERT_FF = 1408
N_DENSE = (DEPTH + 1) // 2
N_MOE = DEPTH // 2

kernel_name = 'hybrid_pool_hyena_mla_diffusion_step'


def rmsnorm(x, g):
    xf = x.astype(jnp.float32)
    y = xf * lax.rsqrt(jnp.mean(xf * xf, axis=-1, keepdims=True) + EPS)
    return (y * g.astype(jnp.float32)).astype(x.dtype)


def pool_mixer(u, w_g, scale):
    b, l, _ = u.shape
    uf = u.astype(jnp.float32)
    cs = jnp.concatenate([jnp.zeros_like(uf[:, :1]), jnp.cumsum(uf, axis=1)], axis=1)
    t = jnp.arange(l)
    outs = []
    for gi, w in enumerate(POOL_WINDOWS):
        lo = jnp.clip(t - w // 2, 0, l)
        hi = jnp.clip(t + w - w // 2, 0, l)
        csg = cs[..., gi * POOL_GC:(gi + 1) * POOL_GC]
        s = jnp.take(csg, hi, axis=1) - jnp.take(csg, lo, axis=1)
        outs.append(s / (hi - lo).astype(jnp.float32)[None, :, None])
    pooled = (jnp.concatenate(outs, axis=-1) - uf).astype(u.dtype)
    mixed = jnp.einsum('blgc,gcd->blgd', pooled.reshape(b, l, POOL_GROUPS, POOL_GC), w_g)
    return mixed.reshape(b, l, POOL_WIDTH) * scale


def short_conv(u, w, bias):
    l = u.shape[1]
    pad = HY_SHORT // 2
    up = jnp.pad(u, ((0, 0), (pad, HY_SHORT - 1 - pad), (0, 0)))
    out = bias
    for j in range(HY_SHORT):
        out = out + up[:, j:j + l] * w[j]
    return out


def hyena_filters(l, w1, b1, w2, b2, w3, freq, decay):
    f32 = jnp.float32
    t = jnp.linspace(0.0, 1.0, l, dtype=f32)[:, None]
    wpos = (2.0 * math.pi / l) * jnp.arange(l, dtype=f32)[:, None]
    bands = jnp.linspace(1e-4, HY_BANDS - 1, HY_BANDS, dtype=f32)[None, :]
    z = jnp.concatenate([t, jnp.cos(bands * wpos), -jnp.sin(bands * wpos)], axis=-1)
    fr = freq.astype(f32)
    hdn = jnp.sin(fr * (z @ w1.astype(f32) + b1.astype(f32)))
    hdn = jnp.sin(fr * (hdn @ w2.astype(f32) + b2.astype(f32)))
    filt = (hdn @ w3.astype(f32)).reshape(l, 2, HY_WIDTH)
    filt = filt * jnp.exp(-t * jnp.abs(decay.astype(f32)))[:, None, :]
    return filt[:, 0], filt[:, 1]


def bidir_long_conv(u, h_f, h_b, bias):
    l = u.shape[1]
    k = jnp.concatenate([h_f, jnp.zeros_like(h_f[:1]), h_b[:0:-1]], axis=0)
    uf = u.astype(jnp.float32)
    spec = jnp.fft.rfft(uf, n=2 * l, axis=1) * jnp.fft.rfft(k, axis=0)[None]
    y = jnp.fft.irfft(spec, n=2 * l, axis=1)[:, :l]
    return (y + uf * bias.astype(jnp.float32)).astype(u.dtype)


def hyena_mixer(u, lp):
    l = u.shape[1]
    z = short_conv(u, lp['hy_conv_w'], lp['hy_conv_b'])
    x0, x1, v = jnp.split(z, 3, axis=-1)
    h_f, h_b = hyena_filters(l, lp['hy_w1'], lp['hy_b1'], lp['hy_w2'], lp['hy_b2'],
                             lp['hy_w3'], lp['hy_freq'], lp['hy_decay'])
    y = bidir_long_conv(v * x1, h_f, h_b, lp['hy_bias'])
    return y * x0


def axial_rope_tables(rows):
    f32 = jnp.float32
    row = jnp.repeat(jnp.arange(rows, dtype=f32), GRID_W)
    col = jnp.tile(jnp.arange(GRID_W, dtype=f32), rows)
    a = ROPE_DIM // 2
    inv = jnp.power(ROPE_BASE, -jnp.arange(0, a, 2, dtype=f32) / a)
    ang = jnp.stack([row, col])[:, :, None] * inv
    return jnp.cos(ang), jnp.sin(ang)


def rope_2d(x, cos, sin):
    a = ROPE_DIM // 2
    hq = a // 2
    cos = cos.astype(x.dtype)
    sin = sin.astype(x.dtype)
    parts = [x[..., :NOPE_DIM]]
    for i in range(2):
        seg = x[..., NOPE_DIM + i * a:NOPE_DIM + (i + 1) * a]
        s1, s2 = seg[..., :hq], seg[..., hq:]
        c = cos[i][None, :, None, :]
        s = sin[i][None, :, None, :]
        parts += [s1 * c - s2 * s, s2 * c + s1 * s]
    return jnp.concatenate(parts, axis=-1)


def mla_queries(q_a, q_norm_g, w_qb, qk_q_g):
    b, l, _ = q_a.shape
    q = (rmsnorm(q_a, q_norm_g) @ w_qb).reshape(b, l, N_HEADS, QK_DIM)
    return rmsnorm(q, qk_q_g)


def mla_keys_values(ckv, krope, w_kvb, qk_k_g):
    b, l, _ = ckv.shape
    kv = (ckv @ w_kvb).reshape(b, l, N_HEADS, NOPE_DIM + V_DIM)
    k = jnp.concatenate([kv[..., :NOPE_DIM],
                         jnp.broadcast_to(krope[:, :, None, :], (b, l, N_HEADS, ROPE_DIM))], axis=-1)
    return rmsnorm(k, qk_k_g), kv[..., NOPE_DIM:]


def attention(q, k, v):
    b, lq, h, _ = q.shape
    nb = lq // Q_BLOCK
    qb = q.reshape(b, nb, Q_BLOCK, h, QK_DIM).transpose(1, 0, 2, 3, 4)
    scale = QK_DIM ** -0.5

    def one_block(qblk):
        s = jnp.einsum('bqhd,bkhd->bhqk', qblk, k, preferred_element_type=jnp.float32) * scale
        p = jax.nn.softmax(s, axis=-1)
        return jnp.einsum('bhqk,bkhd->bqhd', p.astype(v.dtype), v)

    o = lax.map(one_block, qb)
    return o.transpose(1, 0, 2, 3, 4).reshape(b, lq, h * V_DIM)


def token_mixer(h, lp, latent_ctx):
    b, l, _ = h.shape
    proj = h @ lp['w_in']
    c1 = POOL_WIDTH
    c2 = c1 + 3 * HY_WIDTH
    c3 = c2 + Q_RANK
    c4 = c3 + KV_RANK + ROPE_DIM
    u_pool, u_hy, q_a, kv_a, g = jnp.split(proj, [c1, c2, c3, c4], axis=-1)
    y_pool = pool_mixer(u_pool, lp['pool_w'], lp['pool_scale']) @ lp['pool_out']
    y_hy = hyena_mixer(u_hy, lp) @ lp['hy_out']
    ckv = rmsnorm(kv_a[..., :KV_RANK], lp['kv_norm_g'])
    krope = kv_a[..., KV_RANK:]
    q = mla_queries(q_a, lp['q_norm_g'], lp['w_qb'], lp['qk_q_g'])
    k, v = mla_keys_values(ckv, krope, lp['w_kvb'], lp['qk_k_g'])
    if latent_ctx is not None:
        ctx_ckv, ctx_krope, cos, sin = latent_ctx
        q = rope_2d(q, cos, sin)
        k = rope_2d(k, cos, sin)
        k_ctx, v_ctx = mla_keys_values(ctx_ckv, ctx_krope, lp['w_kvb'], lp['qk_k_g'])
        k = jnp.concatenate([k_ctx, k], axis=1)
        v = jnp.concatenate([v_ctx, v], axis=1)
    y_att = attention(q, k, v) @ lp['mla_out']
    gates = jax.nn.sigmoid(g.astype(jnp.float32)).astype(h.dtype).reshape(b, l, N_BRANCH, D_MODEL)
    merged = gates[:, :, 0] * y_pool + gates[:, :, 1] * y_hy + gates[:, :, 2] * y_att
    return merged @ lp['w_out'], ckv, krope


def swiglu(h, w1, w3, w2):
    return (jax.nn.silu(h @ w1) * (h @ w3)) @ w2


def moe_ffn(h, router, w1, w3, w2):
    b, l, d = h.shape
    t = h.reshape(-1, d)
    logits = (t @ router).astype(jnp.float32)
    top_v, top_i = lax.top_k(logits, TOP_K)
    wts = jax.nn.softmax(top_v, axis=-1)
    dense_w = jnp.sum(jax.nn.one_hot(top_i, N_EXPERTS, dtype=jnp.float32) * wts[..., None], axis=1)
    dense_w = dense_w.astype(t.dtype)
    out = jnp.zeros_like(t)
    for e in range(N_EXPERTS):
        out = out + dense_w[:, e:e + 1] * swiglu(t, w1[e], w3[e], w2[e])
    return out.reshape(b, l, d)


def trunk_layer(x, cond, lp, moe_layer, latent_ctx):
    mod = jax.nn.silu(cond) @ lp['w_mod'] + lp['b_mod']
    sh1, sc1, g1, sh2, sc2, g2 = jnp.split(mod[:, None, :], 6, axis=-1)
    h = rmsnorm(x, lp['norm1_g']) * (1 + sc1) + sh1
    mix, ckv, krope = token_mixer(h, lp, latent_ctx)
    x = x + g1 * mix
    h = rmsnorm(x, lp['norm2_g']) * (1 + sc2) + sh2
    if moe_layer:
        f = moe_ffn(h, lp['router'], lp['f_w1'], lp['f_w3'], lp['f_w2'])
    else:
        f = swiglu(h, lp['f_w1'], lp['f_w3'], lp['f_w2'])
    return x + g2 * f, ckv, krope


def setup_inputs(seed: int = 0) -> dict:
    key = jax.random.key(seed)
    ks = iter(jax.random.split(key, 64))
    D = D_MODEL

    def nrm(shape, scale=1.0):
        return scale * jax.random.normal(next(ks), shape, jnp.float32)

    def gain(shape):
        return 1.0 + 0.02 * jax.random.normal(next(ks), shape, jnp.float32)

    return {
        'x_prompt': nrm((BATCH, SEQ, D)),
        'x_sample': nrm((DEC_BATCH, DEC_SEQ, D)),
        'cache_ckv': nrm((DEC_BATCH, DEPTH, PAST_LEN, KV_RANK)),
        'cache_krope': nrm((DEC_BATCH, DEPTH, PAST_LEN, ROPE_DIM)),
        'c': nrm((DEC_BATCH, D)),
        'c_ctx': nrm((D,)),
        'w_mod': nrm((DEPTH, D, 6 * D), 0.5 * D ** -0.5),
        'b_mod': nrm((DEPTH, 6 * D), 0.02),
        'norm1_g': gain((DEPTH, D)),
        'norm2_g': gain((DEPTH, D)),
        'w_in': nrm((DEPTH, D, IN_COLS), D ** -0.5),
        'pool_w': nrm((DEPTH, POOL_GROUPS, POOL_GC, POOL_GC), POOL_GC ** -0.5),
        'pool_scale': 1.0 + 0.1 * jax.random.normal(next(ks), (DEPTH, POOL_WIDTH), jnp.float32),
        'pool_out': nrm((DEPTH, POOL_WIDTH, D), POOL_WIDTH ** -0.5),
        'hy_conv_w': nrm((DEPTH, HY_SHORT, 3 * HY_WIDTH), HY_SHORT ** -0.5),
        'hy_conv_b': nrm((DEPTH, 3 * HY_WIDTH), 0.02),
        'hy_w1': nrm((DEPTH, HY_EMB, HY_FFN), HY_EMB ** -0.5),
        'hy_b1': nrm((DEPTH, HY_FFN), 0.1),
        'hy_w2': nrm((DEPTH, HY_FFN, HY_FFN), HY_FFN ** -0.5),
        'hy_b2': nrm((DEPTH, HY_FFN), 0.1),
        'hy_w3': nrm((DEPTH, HY_FFN, 2 * HY_WIDTH), 0.1 * HY_FFN ** -0.5),
        'hy_freq': 1.0 + 0.1 * jax.random.normal(next(ks), (DEPTH, HY_FFN), jnp.float32),
        'hy_decay': jnp.exp(jax.random.uniform(next(ks), (DEPTH, HY_WIDTH), jnp.float32,
                                               math.log(3.0), math.log(30.0))),
        'hy_bias': nrm((DEPTH, HY_WIDTH)),
        'hy_out': nrm((DEPTH, HY_WIDTH, D), HY_WIDTH ** -0.5),
        'q_norm_g': gain((DEPTH, Q_RANK)),
        'w_qb': nrm((DEPTH, Q_RANK, N_HEADS * QK_DIM), Q_RANK ** -0.5),
        'kv_norm_g': gain((DEPTH, KV_RANK)),
        'w_kvb': nrm((DEPTH, KV_RANK, N_HEADS * (NOPE_DIM + V_DIM)), KV_RANK ** -0.5),
        'qk_q_g': gain((DEPTH, QK_DIM)),
        'qk_k_g': gain((DEPTH, QK_DIM)),
        'mla_out': nrm((DEPTH, N_HEADS * V_DIM, D), (N_HEADS * V_DIM) ** -0.5),
        'w_out': nrm((DEPTH, D, D), D ** -0.5),
        'ffn_w1': nrm((N_DENSE, D, FF_DIM), D ** -0.5),
        'ffn_w3': nrm((N_DENSE, D, FF_DIM), D ** -0.5),
        'ffn_w2': nrm((N_DENSE, FF_DIM, D), FF_DIM ** -0.5),
        'moe_router': nrm((N_MOE, D, N_EXPERTS), D ** -0.5),
        'moe_w1': nrm((N_MOE, N_EXPERTS, D, EXPERT_FF), D ** -0.5),
        'moe_w3': nrm((N_MOE, N_EXPERTS, D, EXPERT_FF), D ** -0.5),
        'moe_w2': nrm((N_MOE, N_EXPERTS, EXPERT_FF, D), EXPERT_FF ** -0.5),
    }


def reference(x_prompt, x_sample, cache_ckv, cache_krope, c, c_ctx, w_mod, b_mod, norm1_g, norm2_g,
              w_in, pool_w, pool_scale, pool_out, hy_conv_w, hy_conv_b, hy_w1, hy_b1, hy_w2, hy_b2,
              hy_w3, hy_freq, hy_decay, hy_bias, hy_out, q_norm_g, w_qb, kv_norm_g, w_kvb, qk_q_g,
              qk_k_g, mla_out, w_out, ffn_w1, ffn_w3, ffn_w2, moe_router, moe_w1, moe_w3, moe_w2):
    rows = x_sample.shape[1] // GRID_W
    cos, sin = axial_rope_tables(rows)
    y_p = x_prompt
    y_s = x_sample
    ckv_list = []
    kr_list = []
    for i in range(DEPTH):
        j = i // 2
        moe_layer = (i % 2 == 1)
        lp = {
            'w_mod': w_mod[i], 'b_mod': b_mod[i], 'norm1_g': norm1_g[i], 'norm2_g': norm2_g[i],
            'w_in': w_in[i], 'pool_w': pool_w[i], 'pool_scale': pool_scale[i], 'pool_out': pool_out[i],
            'hy_conv_w': hy_conv_w[i], 'hy_conv_b': hy_conv_b[i], 'hy_w1': hy_w1[i], 'hy_b1': hy_b1[i],
            'hy_w2': hy_w2[i], 'hy_b2': hy_b2[i], 'hy_w3': hy_w3[i], 'hy_freq': hy_freq[i],
            'hy_decay': hy_decay[i], 'hy_bias': hy_bias[i], 'hy_out': hy_out[i],
            'q_norm_g': q_norm_g[i], 'w_qb': w_qb[i], 'kv_norm_g': kv_norm_g[i], 'w_kvb': w_kvb[i],
            'qk_q_g': qk_q_g[i], 'qk_k_g': qk_k_g[i], 'mla_out': mla_out[i], 'w_out': w_out[i],
        }
        if moe_layer:
            lp['router'] = moe_router[j]
            lp['f_w1'] = moe_w1[j]
            lp['f_w3'] = moe_w3[j]
            lp['f_w2'] = moe_w2[j]
        else:
            lp['f_w1'] = ffn_w1[j]
            lp['f_w3'] = ffn_w3[j]
            lp['f_w2'] = ffn_w2[j]
        y_p, ckv, kr = trunk_layer(y_p, c_ctx[None, :], lp, moe_layer, None)
        ckv_list.append(ckv)
        kr_list.append(kr)
        y_s, _, _ = trunk_layer(y_s, c, lp, moe_layer, (cache_ckv[:, i], cache_krope[:, i], cos, sin))
    new_ckv = jnp.stack(ckv_list, axis=1)
    new_krope = jnp.stack(kr_list, axis=1)
    return (y_p, y_s, new_ckv, new_krope)
```

```python
import functools
import math

import numpy as np
import jax
import jax.numpy as jnp
from jax import lax
from jax.experimental import pallas as pl
from jax.experimental.pallas import tpu as pltpu

F32 = jnp.float32
MXU_DTYPE = jnp.bfloat16
HIGHEST = lax.Precision.HIGHEST

D_MODEL = 1024
GRID_W = 64
EPS = 1e-6
POOL_WIDTH = 256
POOL_GROUPS = 4
POOL_WINDOWS = (2, 4, 8, 16)
HY_WIDTH = 256
HY_SHORT = 3
HY_BANDS = 8
HY_EMB = 1 + 2 * HY_BANDS
HY_FFN = 64
N_HEADS = 8
Q_RANK = 384
KV_RANK = 256
NOPE_DIM = 64
ROPE_DIM = 32
V_DIM = 64
QK_DIM = NOPE_DIM + ROPE_DIM
ROPE_BASE = 10000.0
N_EXPERTS = 8

LANE = 128
SUBLANE = 8
HEAD_PAD = LANE
Q_PAD = 512
VMEM_LIMIT = 56 * 1024 * 1024

OFF_HY = 0
OFF_POOL = 3 * HY_WIDTH
OFF_KV = OFF_POOL + POOL_WIDTH
OFF_KROPE = OFF_KV + KV_RANK
OFF_Q = 1536
OFF_G = OFF_Q + Q_PAD
IN_PAD = OFF_G + 3 * D_MODEL


def _cp(sem, vmem=VMEM_LIMIT):
    return pltpu.CompilerParams(dimension_semantics=sem, vmem_limit_bytes=vmem)


def _mm(a, b):
    return jnp.dot(a.astype(MXU_DTYPE), b.astype(MXU_DTYPE), preferred_element_type=F32)


def _mm_f32(a, b):
    return jnp.dot(a, b, preferred_element_type=F32, precision=HIGHEST)


def _sigmoid(x):
    return 1.0 / (1.0 + jnp.exp(-x))


def _silu(x):
    return x * _sigmoid(x)


def _rms(x, n):
    ms = jnp.sum(x * x, axis=-1, keepdims=True) * (1.0 / n)
    return x * lax.rsqrt(ms + EPS)


def _mod_row_map(tm, ctx_rows, dec_seq):
    def index_map(i, *_):
        row0 = i * tm
        return (jnp.where(row0 < ctx_rows, 0, 1 + (row0 - ctx_rows) // dec_seq), 0, 0)
    return index_map


def _mod_kernel(c_ref, w_ref, b_ref, o_ref):
    o_ref[0] = _mm_f32(_silu(c_ref[...]), w_ref[0]) + b_ref[0]


def _modulation(cond, w_mod, b_mod):
    depth, d, n6 = w_mod.shape
    r = cond.shape[0]
    tn = 1024
    return pl.pallas_call(
        _mod_kernel,
        grid=(depth, n6 // tn),
        in_specs=[pl.BlockSpec((r, d), lambda l, j: (0, 0)),
                  pl.BlockSpec((1, d, tn), lambda l, j: (l, 0, j)),
                  pl.BlockSpec((1, 1, tn), lambda l, j: (l, 0, j))],
        out_specs=pl.BlockSpec((1, r, tn), lambda l, j: (l, 0, j)),
        out_shape=jax.ShapeDtypeStruct((depth, r, n6), F32),
        compiler_params=_cp(("parallel", "parallel")),
        name="modulation",
    )(cond, w_mod, b_mod.reshape(depth, 1, n6))


def _inproj_kernel(x_ref, mod_ref, g_ref, w_ref, o_ref, h_ref):
    @pl.when(pl.program_id(1) == 0)
    def _():
        m = mod_ref[0]
        h = _rms(x_ref[...], D_MODEL) * g_ref[...]
        h_ref[...] = (h * (1.0 + m[1:2]) + m[0:1]).astype(h_ref.dtype)
    o_ref[...] = jnp.dot(h_ref[...], w_ref[...], preferred_element_type=F32)


def _inproj(x, mod, norm_g, w, tm, ctx_rows, dec_seq):
    nt, d = x.shape
    n = w.shape[1]
    tn = 512
    return pl.pallas_call(
        _inproj_kernel,
        grid=(nt // tm, n // tn),
        in_specs=[pl.BlockSpec((tm, d), lambda i, j: (i, 0)),
                  pl.BlockSpec((1, 6, d), _mod_row_map(tm, ctx_rows, dec_seq)),
                  pl.BlockSpec((1, d), lambda i, j: (0, 0)),
                  pl.BlockSpec((d, tn), lambda i, j: (0, j))],
        out_specs=pl.BlockSpec((tm, tn), lambda i, j: (i, j)),
        out_shape=jax.ShapeDtypeStruct((nt, n), F32),
        scratch_shapes=[pltpu.VMEM((tm, d), MXU_DTYPE)],
        compiler_params=_cp(("parallel", "arbitrary")),
        name="inproj",
    )(x, mod, norm_g, w)


def _local_kernel(u_ref, prev_ref, next_ref, cw_ref, cb_ref, pw_ref, ps_ref,
                  vx_ref, vxb_ref, x0_ref, p_ref, ext_ref, *, tl, n_ctx_tiles, seq, dec_seq):
    i = pl.program_id(0)
    row0 = i * tl
    ctx = i < n_ctx_tiles
    p0 = jnp.where(ctx, row0 % seq, (row0 - n_ctx_tiles * tl) % dec_seq)
    length = jnp.where(ctx, seq, dec_seq)
    first = p0 == 0
    last = p0 + tl == length

    ext_ref[0:SUBLANE, :] = jnp.where(first, 0.0, prev_ref[...])
    ext_ref[SUBLANE:SUBLANE + tl, :] = u_ref[...]
    ext_ref[SUBLANE + tl:, :] = jnp.where(last, 0.0, next_ref[...])

    hw = 3 * HY_WIDTH
    cw = cw_ref[...]
    z = (cb_ref[...]
         + ext_ref[pl.ds(SUBLANE - 1, tl), 0:hw] * cw[0:1]
         + ext_ref[pl.ds(SUBLANE, tl), 0:hw] * cw[1:2]
         + ext_ref[pl.ds(SUBLANE + 1, tl), 0:hw] * cw[2:3])
    x0 = z[:, 0:HY_WIDTH]
    vx = z[:, 2 * HY_WIDTH:3 * HY_WIDTH] * z[:, HY_WIDTH:2 * HY_WIDTH]
    vx_ref[...] = vx
    vxb_ref[...] = vx.astype(vxb_ref.dtype)
    x0_ref[...] = x0

    def e(j):
        return ext_ref[pl.ds(SUBLANE + j, tl), hw:hw + POOL_WIDTH]

    u = e(0)
    s2 = e(-1) + u
    s4 = s2 + e(-2) + e(1)
    s8 = s4 + e(-4) + e(-3) + e(2) + e(3)
    s16 = s8 + e(-8) + e(-7) + e(-6) + e(-5) + e(4) + e(5) + e(6) + e(7)
    lane = lax.broadcasted_iota(jnp.int32, (tl, POOL_WIDTH), 1)
    gc = POOL_WIDTH // POOL_GROUPS
    ssum = jnp.where(lane < gc, s2, jnp.where(lane < 2 * gc, s4, jnp.where(lane < 3 * gc, s8, s16)))
    half = jnp.where(lane < gc, 1, jnp.where(lane < 2 * gc, 2, jnp.where(lane < 3 * gc, 4, 8)))
    t = p0 + lax.broadcasted_iota(jnp.int32, (tl, POOL_WIDTH), 0)
    cnt = jnp.minimum(t + half, length) - jnp.maximum(t - half, 0)
    pooled = ssum / cnt.astype(F32) - u
    p_ref[...] = (_mm(pooled, pw_ref[...]) * ps_ref[...]).astype(p_ref.dtype)


def _local_mixers(proj, conv_w, conv_b, pool_bd, pool_scale, tl, n_ctx_tiles, seq, dec_seq):
    nt = proj.shape[0]
    width = 3 * HY_WIDTH + POOL_WIDTH
    nb = tl // SUBLANE
    last_blk = nt // SUBLANE - 1
    kern = functools.partial(_local_kernel, tl=tl, n_ctx_tiles=n_ctx_tiles, seq=seq, dec_seq=dec_seq)
    row = lambda i: (i, 0)
    const = lambda i: (0, 0)
    return pl.pallas_call(
        kern,
        grid=(nt // tl,),
        in_specs=[pl.BlockSpec((tl, width), row),
                  pl.BlockSpec((SUBLANE, width), lambda i: (jnp.maximum(i * nb - 1, 0), 0)),
                  pl.BlockSpec((SUBLANE, width), lambda i: (jnp.minimum((i + 1) * nb, last_blk), 0)),
                  pl.BlockSpec((HY_SHORT, 3 * HY_WIDTH), const),
                  pl.BlockSpec((1, 3 * HY_WIDTH), const),
                  pl.BlockSpec((POOL_WIDTH, POOL_WIDTH), const),
                  pl.BlockSpec((1, POOL_WIDTH), const)],
        out_specs=[pl.BlockSpec((tl, HY_WIDTH), row)] * 3 + [pl.BlockSpec((tl, POOL_WIDTH), row)],
        out_shape=[jax.ShapeDtypeStruct((nt, HY_WIDTH), F32),
                   jax.ShapeDtypeStruct((nt, HY_WIDTH), MXU_DTYPE),
                   jax.ShapeDtypeStruct((nt, HY_WIDTH), F32),
                   jax.ShapeDtypeStruct((nt, POOL_WIDTH), MXU_DTYPE)],
        scratch_shapes=[pltpu.VMEM((tl + 2 * SUBLANE, width), F32)],
        compiler_params=_cp(("parallel",)),
        name="local_mixers",
    )(proj, proj, proj, conv_w, conv_b, pool_bd, pool_scale)


def _filter_kernel(z_ref, w1_ref, b1_ref, w2_ref, b2_ref, w3_ref, fr_ref, dec_ref, o_ref):
    z = z_ref[...]
    fr = fr_ref[0]
    hdn = jnp.sin(fr * (_mm_f32(z, w1_ref[0]) + b1_ref[0]))
    hdn = jnp.sin(fr * (_mm_f32(hdn, w2_ref[0]) + b2_ref[0]))
    filt = _mm_f32(hdn, w3_ref[0])
    win = jnp.exp(-z[:, 0:1] * jnp.abs(dec_ref[0]))
    o_ref[0] = filt * jnp.concatenate([win, win], axis=-1)


def _hyena_filters(length, w1, b1, w2, b2, w3, freq, decay):
    depth = w1.shape[0]
    t = np.linspace(0.0, 1.0, length, dtype=np.float32)[:, None]
    wpos = (np.float32(2.0 * math.pi / length) * np.arange(length, dtype=np.float32))[:, None]
    bands = np.linspace(1e-4, HY_BANDS - 1, HY_BANDS, dtype=np.float32)[None, :]
    emb_pad = 32
    z = np.zeros((length, emb_pad), np.float32)
    z[:, 0:1] = t
    z[:, 1:1 + HY_BANDS] = np.cos(bands * wpos)
    z[:, 1 + HY_BANDS:HY_EMB] = -np.sin(bands * wpos)
    w1p = jnp.pad(w1, ((0, 0), (0, emb_pad - HY_EMB), (0, 0)))
    tl = min(length, 512)
    lay = lambda l, i: (l, 0, 0)
    r3 = lambda a: a.reshape(depth, 1, a.shape[-1])
    return pl.pallas_call(
        _filter_kernel,
        grid=(depth, length // tl),
        in_specs=[pl.BlockSpec((tl, emb_pad), lambda l, i: (i, 0)),
                  pl.BlockSpec((1, emb_pad, HY_FFN), lay),
                  pl.BlockSpec((1, 1, HY_FFN), lay),
                  pl.BlockSpec((1, HY_FFN, HY_FFN), lay),
                  pl.BlockSpec((1, 1, HY_FFN), lay),
                  pl.BlockSpec((1, HY_FFN, 2 * HY_WIDTH), lay),
                  pl.BlockSpec((1, 1, HY_FFN), lay),
                  pl.BlockSpec((1, 1, HY_WIDTH), lay)],
        out_specs=pl.BlockSpec((1, tl, 2 * HY_WIDTH), lambda l, i: (l, i, 0)),
        out_shape=jax.ShapeDtypeStruct((depth, length, 2 * HY_WIDTH), F32),
        compiler_params=_cp(("parallel", "parallel")),
        name="hyena_filters",
    )(jnp.asarray(z), w1p, r3(b1), w2, r3(b2), w3, r3(freq), r3(decay))


def _dft_matrices(length):
    n = 2 * length
    f = jnp.arange(length, dtype=jnp.int32)[:, None]
    s = jnp.arange(length, dtype=jnp.int32)[None, :]
    ang = ((f * s) % n).astype(F32) * np.float32(2.0 * math.pi / n)
    cos = jnp.cos(ang)
    sin = jnp.sin(ang)
    alt = jnp.where(jnp.arange(length) % 2 == 0, 1.0, -1.0).astype(F32)
    fwd_im = jnp.where(f == 0, alt[None, :], -sin)
    fwd = jnp.concatenate([cos, fwd_im], axis=0)
    inv_re = jnp.where(s == 0, 1.0 / n, (2.0 / n) * cos)
    inv_im = jnp.where(s == 0, alt[:, None] / n, (-2.0 / n) * sin)
    inv = jnp.concatenate([inv_re, inv_im], axis=1)
    return fwd.astype(MXU_DTYPE), inv.astype(MXU_DTYPE)


def _dft_fwd_kernel(g_ref, x_ref, o_ref, *, bg):
    @pl.when(pl.program_id(2) == 0)
    def _():
        o_ref[...] = jnp.zeros_like(o_ref)
    g = g_ref[...]
    for b in range(bg):
        o_ref[b] += jnp.dot(g, x_ref[b], preferred_element_type=F32)


def _dft_fwd(gmat, x, bg):
    bsz, length, cols = x.shape
    tm = min(2 * length, 1024)
    tk = min(length, 512)
    return pl.pallas_call(
        functools.partial(_dft_fwd_kernel, bg=bg),
        grid=(bsz // bg, 2 * length // tm, length // tk),
        in_specs=[pl.BlockSpec((tm, tk), lambda g, m, k: (m, k)),
                  pl.BlockSpec((bg, tk, cols), lambda g, m, k: (g, k, 0))],
        out_specs=pl.BlockSpec((bg, tm, cols), lambda g, m, k: (g, m, 0)),
        out_shape=jax.ShapeDtypeStruct((bsz, 2 * length, cols), F32),
        compiler_params=_cp(("parallel", "parallel", "arbitrary")),
        name="dft_fwd",
    )(gmat, x)


def _dft_inv_kernel(gc_ref, gs_ref, xr_ref, xi_ref, ar_ref, ai_ref, hb0_ref, vx_ref, x0_ref,
                    bias_ref, o_ref, acc_ref, *, bg, tk):
    k = pl.program_id(2)

    @pl.when(k == 0)
    def _():
        acc_ref[...] = jnp.zeros_like(acc_ref)

    c = HY_WIDTH
    ar = ar_ref[...]
    ai = ai_ref[...]
    hb0 = hb0_ref[...]
    f0 = (k * tk + lax.broadcasted_iota(jnp.int32, (tk, c), 0)) == 0
    kr = ar[:, 0:c] + ar[:, c:2 * c] - hb0
    ki = jnp.where(f0, ai[:, 0:c] + ai[:, c:2 * c] - hb0, ai[:, 0:c] - ai[:, c:2 * c])
    gc = gc_ref[...]
    gs = gs_ref[...]
    for b in range(bg):
        xr = xr_ref[b]
        xi = xi_ref[b]
        zr = jnp.where(f0, xr * kr, xr * kr - xi * ki)
        zi = jnp.where(f0, xi * ki, xr * ki + xi * kr)
        acc_ref[b] += (jnp.dot(gc, zr.astype(MXU_DTYPE), preferred_element_type=F32)
                       + jnp.dot(gs, zi.astype(MXU_DTYPE), preferred_element_type=F32))

    @pl.when(k == pl.num_programs(2) - 1)
    def _():
        y = (acc_ref[...] + vx_ref[...] * bias_ref[...]) * x0_ref[...]
        o_ref[...] = y.astype(o_ref.dtype)


def _dft_inv(ginv, xspec, fspec, hb0, vx, x0, bias, bg):
    bsz, length, c = vx.shape
    tm = min(length, 512)
    tk = min(length, 512)
    nk = length // tk
    return pl.pallas_call(
        functools.partial(_dft_inv_kernel, bg=bg, tk=tk),
        grid=(bsz // bg, length // tm, nk),
        in_specs=[pl.BlockSpec((tm, tk), lambda g, m, k: (m, k)),
                  pl.BlockSpec((tm, tk), lambda g, m, k: (m, nk + k)),
                  pl.BlockSpec((bg, tk, c), lambda g, m, k: (g, k, 0)),
                  pl.BlockSpec((bg, tk, c), lambda g, m, k: (g, nk + k, 0)),
                  pl.BlockSpec((tk, 2 * c), lambda g, m, k: (k, 0)),
                  pl.BlockSpec((tk, 2 * c), lambda g, m, k: (nk + k, 0)),
                  pl.BlockSpec((1, c), lambda g, m, k: (0, 0)),
                  pl.BlockSpec((bg, tm, c), lambda g, m, k: (g, m, 0)),
                  pl.BlockSpec((bg, tm, c), lambda g, m, k: (g, m, 0)),
                  pl.BlockSpec((1, c), lambda g, m, k: (0, 0))],
        out_specs=pl.BlockSpec((bg, tm, c), lambda g, m, k: (g, m, 0)),
        out_shape=jax.ShapeDtypeStruct((bsz, length, c), MXU_DTYPE),
        scratch_shapes=[pltpu.VMEM((bg, tm, c), F32)],
        compiler_params=_cp(("parallel", "parallel", "arbitrary")),
        name="dft_inv",
    )(ginv, ginv, xspec, xspec, fspec, fspec, hb0, vx, x0, bias)


def _long_conv(vx, vxb, x0, filt, bias, mats, bsz, length):
    gfwd, ginv = mats
    c = vx.shape[-1]
    bg = math.gcd(bsz, 8)
    fspec = _dft_fwd(gfwd, filt.astype(MXU_DTYPE)[None], 1)[0]
    xspec = _dft_fwd(gfwd, vxb.reshape(bsz, length, c), bg)
    hb0 = filt[0:1, c:2 * c]
    out = _dft_inv(ginv, xspec, fspec, hb0, vx.reshape(bsz, length, c), x0.reshape(bsz, length, c),
                   bias, bg)
    return out.reshape(bsz * length, c)


def _rope(x, cos, sgn_sin):
    lane = lax.broadcasted_iota(jnp.int32, x.shape, 1) % (ROPE_DIM // 2)
    lower = lane < ROPE_DIM // 4
    partner = jnp.where(lower, pltpu.roll(x, HEAD_PAD - ROPE_DIM // 4, 1), pltpu.roll(x, ROPE_DIM // 4, 1))
    return x * cos + partner * sgn_sin


def _head_norm_rope(xh, gain, cos, sgn_sin):
    return _rope(_rms(xh, QK_DIM) * gain, cos, sgn_sin)


def _kv_kernel(kva_ref, kr_ref, ng_ref, wk_ref, wv_ref, kg_ref, cos_ref, sin_ref,
               ckv_ref, k_ref, v_ref, *, normalize):
    kva = kva_ref[...]
    ckv = _rms(kva, KV_RANK) * ng_ref[...] if normalize else kva
    if normalize:
        ckv_ref[...] = ckv
    cb = ckv.astype(MXU_DTYPE)
    kfull = jnp.dot(cb, wk_ref[...], preferred_element_type=F32)
    kr = kr_ref[...]
    cos = cos_ref[...]
    sin = sin_ref[...]
    kg = kg_ref[...]
    for h in range(N_HEADS):
        sl = slice(h * HEAD_PAD, (h + 1) * HEAD_PAD)
        k_ref[:, sl] = _head_norm_rope(kfull[:, sl] + kr, kg, cos, sin).astype(k_ref.dtype)
    v_ref[...] = jnp.dot(cb, wv_ref[...], preferred_element_type=F32).astype(v_ref.dtype)


def _kv_proj(src, kv_blk, kr_src, kr_blk, norm_g, wk, wv, kg, cos, sin, table_map, tl, normalize):
    nt = src.shape[0]
    const = lambda i: (0, 0)
    row = lambda i: (i, 0)
    out_shape = [jax.ShapeDtypeStruct((nt, KV_RANK), F32),
                 jax.ShapeDtypeStruct((nt, N_HEADS * HEAD_PAD), MXU_DTYPE),
                 jax.ShapeDtypeStruct((nt, N_HEADS * V_DIM), MXU_DTYPE)]
    out_specs = [pl.BlockSpec((tl, KV_RANK), row),
                 pl.BlockSpec((tl, N_HEADS * HEAD_PAD), row),
                 pl.BlockSpec((tl, N_HEADS * V_DIM), row)]
    kern = functools.partial(_kv_kernel, normalize=normalize)
    if not normalize:
        out_shape, out_specs = out_shape[1:], out_specs[1:]
        kern = lambda *refs: _kv_kernel(*refs[:8], None, *refs[8:], normalize=False)
    return pl.pallas_call(
        kern,
        grid=(nt // tl,),
        in_specs=[pl.BlockSpec((tl, KV_RANK), lambda i: (i, kv_blk)),
                  pl.BlockSpec((tl, HEAD_PAD), lambda i: (i, kr_blk)),
                  pl.BlockSpec((1, KV_RANK), const),
                  pl.BlockSpec((KV_RANK, N_HEADS * HEAD_PAD), const),
                  pl.BlockSpec((KV_RANK, N_HEADS * V_DIM), const),
                  pl.BlockSpec((1, HEAD_PAD), const),
                  pl.BlockSpec((tl, HEAD_PAD), table_map),
                  pl.BlockSpec((tl, HEAD_PAD), table_map)],
        out_specs=out_specs,
        out_shape=out_shape,
        compiler_params=_cp(("parallel",)),
        name="kv_proj" if normalize else "kv_proj_cache",
    )(src, kr_src, norm_g, wk, wv, kg, cos, sin)


def _q_kernel(qa_ref, ng_ref, wq_ref, qg_ref, cos_ref, sin_ref, q_ref):
    qn = _rms(qa_ref[...], Q_RANK) * ng_ref[...]
    qfull = jnp.dot(qn.astype(MXU_DTYPE), wq_ref[...], preferred_element_type=F32)
    cos = cos_ref[...]
    sin = sin_ref[...]
    qg = qg_ref[...]
    for h in range(N_HEADS):
        sl = slice(h * HEAD_PAD, (h + 1) * HEAD_PAD)
        q_ref[:, sl] = _head_norm_rope(qfull[:, sl], qg, cos, sin).astype(q_ref.dtype)


def _q_proj(proj, norm_g, wq, qg, cos, sin, table_map, tl):
    nt = proj.shape[0]
    const = lambda i: (0, 0)
    return pl.pallas_call(
        _q_kernel,
        grid=(nt // tl,),
        in_specs=[pl.BlockSpec((tl, Q_PAD), lambda i: (i, OFF_Q // Q_PAD)),
                  pl.BlockSpec((1, Q_PAD), const),
                  pl.BlockSpec((Q_PAD, N_HEADS * HEAD_PAD), const),
                  pl.BlockSpec((1, HEAD_PAD), const),
                  pl.BlockSpec((tl, HEAD_PAD), table_map),
                  pl.BlockSpec((tl, HEAD_PAD), table_map)],
        out_specs=pl.BlockSpec((tl, N_HEADS * HEAD_PAD), lambda i: (i, 0)),
        out_shape=jax.ShapeDtypeStruct((nt, N_HEADS * HEAD_PAD), MXU_DTYPE),
        compiler_params=_cp(("parallel",)),
        name="q_proj",
    )(proj, norm_g, wq, qg, cos, sin)


def _rope_tables(tl, dec_seq):
    a = ROPE_DIM // 2
    t = np.arange(dec_seq)
    pos = np.stack([t // GRID_W, t % GRID_W]).astype(np.float32)
    inv = np.power(np.float32(ROPE_BASE), -np.arange(0, a, 2, dtype=np.float32) / np.float32(a))
    ang = pos[:, :, None] * inv.astype(np.float32)
    cos = np.ones((tl + dec_seq, HEAD_PAD), np.float32)
    sin = np.zeros((tl + dec_seq, HEAD_PAD), np.float32)
    hq = a // 2
    for i in range(2):
        lo = NOPE_DIM + i * a
        cos[tl:, lo:lo + hq] = np.cos(ang[i])
        cos[tl:, lo + hq:lo + a] = np.cos(ang[i])
        sin[tl:, lo:lo + hq] = -np.sin(ang[i])
        sin[tl:, lo + hq:lo + a] = np.sin(ang[i])
    return jnp.asarray(cos), jnp.asarray(sin)


def _attn_kernel(*refs, has_ctx):
    if has_ctx:
        q_ref, k_ref, v_ref, kc_ref, vc_ref, o_ref = refs
    else:
        q_ref, k_ref, v_ref, o_ref = refs
    c = (QK_DIM ** -0.5) * math.log2(math.e)
    nt_dims = (((1,), (1,)), ((), ()))
    outs = []
    for h in range(2):
        qs = slice(h * HEAD_PAD, (h + 1) * HEAD_PAD)
        vs = slice(h * V_DIM, (h + 1) * V_DIM)
        qh = q_ref[:, qs]
        s = lax.dot_general(qh, k_ref[:, qs], nt_dims, preferred_element_type=F32)
        m = jnp.max(s, axis=-1, keepdims=True)
        if has_ctx:
            sc = lax.dot_general(qh, kc_ref[:, qs], nt_dims, preferred_element_type=F32)
            m = jnp.maximum(m, jnp.max(sc, axis=-1, keepdims=True))
        p = jnp.exp2((s - m) * c)
        l = jnp.sum(p, axis=-1, keepdims=True)
        o = jnp.dot(p.astype(MXU_DTYPE), v_ref[:, vs], preferred_element_type=F32)
        if has_ctx:
            pc = jnp.exp2((sc - m) * c)
            l = l + jnp.sum(pc, axis=-1, keepdims=True)
            o = o + jnp.dot(pc.astype(MXU_DTYPE), vc_ref[:, vs], preferred_element_type=F32)
        outs.append(o / l)
    o_ref[...] = jnp.concatenate(outs, axis=-1).astype(o_ref.dtype)


def _attention(q, k, v, out_rows, row_off, bsz, length, tq, ctx=None):
    assert row_off % length == 0 and length % tq == 0
    qb0 = row_off // tq
    kb0 = row_off // length
    nq = length // tq
    hp = N_HEADS // 2
    in_specs = [pl.BlockSpec((tq, 2 * HEAD_PAD), lambda b, h, i: (qb0 + b * nq + i, h)),
                pl.BlockSpec((length, 2 * HEAD_PAD), lambda b, h, i: (kb0 + b, h)),
                pl.BlockSpec((length, 2 * V_DIM), lambda b, h, i: (kb0 + b, h))]
    args = [q, k, v]
    if ctx is not None:
        kc, vc, past = ctx
        in_specs += [pl.BlockSpec((past, 2 * HEAD_PAD), lambda b, h, i: (b, h)),
                     pl.BlockSpec((past, 2 * V_DIM), lambda b, h, i: (b, h))]
        args += [kc, vc]
    return pl.pallas_call(
        functools.partial(_attn_kernel, has_ctx=ctx is not None),
        grid=(bsz, hp, nq),
        in_specs=in_specs,
        out_specs=pl.BlockSpec((tq, 2 * V_DIM), lambda b, h, i: (b * nq + i, h)),
        out_shape=jax.ShapeDtypeStruct((out_rows, N_HEADS * V_DIM), MXU_DTYPE),
        compiler_params=_cp(("parallel", "parallel", "arbitrary")),
        name="attention_ctx" if ctx is None else "attention_latent",
    )(*args)


def _merge_kernel(p_ref, h_ref, a_ref, g0_ref, g1_ref, g2_ref, x_ref, mod_ref,
                  wp_ref, wh_ref, wa_ref, wo_ref, o_ref):
    yp = jnp.dot(p_ref[...], wp_ref[...], preferred_element_type=F32)
    yh = jnp.dot(h_ref[...], wh_ref[...], preferred_element_type=F32)
    ya = jnp.dot(a_ref[...], wa_ref[...], preferred_element_type=F32)
    merged = _sigmoid(g0_ref[...]) * yp + _sigmoid(g1_ref[...]) * yh + _sigmoid(g2_ref[...]) * ya
    mix = jnp.dot(merged.astype(MXU_DTYPE), wo_ref[...], preferred_element_type=F32)
    o_ref[...] = x_ref[...] + mod_ref[0][2:3] * mix


def _merge(p, hy, att, proj, x, mod, wp, wh, wa, wo, tm, ctx_rows, dec_seq):
    nt, d = x.shape
    row = lambda i: (i, 0)
    const = lambda i: (0, 0)
    g0 = OFF_G // d
    return pl.pallas_call(
        _merge_kernel,
        grid=(nt // tm,),
        in_specs=[pl.BlockSpec((tm, POOL_WIDTH), row),
                  pl.BlockSpec((tm, HY_WIDTH), row),
                  pl.BlockSpec((tm, N_HEADS * V_DIM), row),
                  pl.BlockSpec((tm, d), lambda i: (i, g0)),
                  pl.BlockSpec((tm, d), lambda i: (i, g0 + 1)),
                  pl.BlockSpec((tm, d), lambda i: (i, g0 + 2)),
                  pl.BlockSpec((tm, d), row),
                  pl.BlockSpec((1, 6, d), _mod_row_map(tm, ctx_rows, dec_seq)),
                  pl.BlockSpec((POOL_WIDTH, d), const),
                  pl.BlockSpec((HY_WIDTH, d), const),
                  pl.BlockSpec((N_HEADS * V_DIM, d), const),
                  pl.BlockSpec((d, d), const)],
        out_specs=pl.BlockSpec((tm, d), row),
        out_shape=jax.ShapeDtypeStruct((nt, d), F32),
        compiler_params=_cp(("parallel",)),
        name="merge",
    )(p, hy, att, proj, proj, proj, x, mod, wp, wh, wa, wo)


def _norm2(x_ref, mod_ref, g_ref):
    m = mod_ref[0]
    return (_rms(x_ref[...], D_MODEL) * g_ref[...]) * (1.0 + m[4:5]) + m[3:4]


def _ffn_kernel(x_ref, mod_ref, g_ref, w1_ref, w3_ref, w2_ref, o_ref, h_ref, acc_ref):
    j = pl.program_id(1)

    @pl.when(j == 0)
    def _():
        h_ref[...] = _norm2(x_ref, mod_ref, g_ref).astype(h_ref.dtype)
        acc_ref[...] = jnp.zeros_like(acc_ref)

    h = h_ref[...]
    a = jnp.dot(h, w1_ref[...], preferred_element_type=F32)
    b = jnp.dot(h, w3_ref[...], preferred_element_type=F32)
    acc_ref[...] += jnp.dot((_silu(a) * b).astype(MXU_DTYPE), w2_ref[...], preferred_element_type=F32)

    @pl.when(j == pl.num_programs(1) - 1)
    def _():
        o_ref[...] = x_ref[...] + mod_ref[0][5:6] * acc_ref[...]


def _ffn_dense(x, mod, norm_g, w1, w3, w2, tm, tf, ctx_rows, dec_seq):
    nt, d = x.shape
    ff = w1.shape[1]
    return pl.pallas_call(
        _ffn_kernel,
        grid=(nt // tm, ff // tf),
        in_specs=[pl.BlockSpec((tm, d), lambda i, j: (i, 0)),
                  pl.BlockSpec((1, 6, d), _mod_row_map(tm, ctx_rows, dec_seq)),
                  pl.BlockSpec((1, d), lambda i, j: (0, 0)),
                  pl.BlockSpec((d, tf), lambda i, j: (0, j)),
                  pl.BlockSpec((d, tf), lambda i, j: (0, j)),
                  pl.BlockSpec((tf, d), lambda i, j: (j, 0))],
        out_specs=pl.BlockSpec((tm, d), lambda i, j: (i, 0)),
        out_shape=jax.ShapeDtypeStruct((nt, d), F32),
        scratch_shapes=[pltpu.VMEM((tm, d), MXU_DTYPE), pltpu.VMEM((tm, d), F32)],
        compiler_params=_cp(("parallel", "arbitrary")),
        name="ffn_dense",
    )(x, mod, norm_g, w1, w3, w2)


def _router_kernel(x_ref, mod_ref, g_ref, r_ref, dw_ref):
    h = _norm2(x_ref, mod_ref, g_ref)
    logits = _mm_f32(h, r_ref[...])
    lane = lax.broadcasted_iota(jnp.int32, logits.shape, 1).astype(F32)
    neg = jnp.float32(-jnp.inf)
    logits = jnp.where(lane < N_EXPERTS, logits, neg)
    m1 = jnp.max(logits, axis=-1, keepdims=True)
    i1 = jnp.min(jnp.where(logits == m1, lane, float(LANE)), axis=-1, keepdims=True)
    rest = jnp.where(lane == i1, neg, logits)
    m2 = jnp.max(rest, axis=-1, keepdims=True)
    i2 = jnp.min(jnp.where(rest == m2, lane, float(LANE)), axis=-1, keepdims=True)
    e2 = jnp.exp(m2 - m1)
    w_top = 1.0 / (1.0 + e2)
    dw_ref[...] = jnp.where(lane == i1, w_top, 0.0) + jnp.where(lane == i2, e2 * w_top, 0.0)


def _router(x, mod, norm_g, router, tm, ctx_rows, dec_seq):
    nt, d = x.shape
    rp = jnp.pad(router, ((0, 0), (0, LANE - N_EXPERTS)))
    return pl.pallas_call(
        _router_kernel,
        grid=(nt // tm,),
        in_specs=[pl.BlockSpec((tm, d), lambda i: (i, 0)),
                  pl.BlockSpec((1, 6, d), _mod_row_map(tm, ctx_rows, dec_seq)),
                  pl.BlockSpec((1, d), lambda i: (0, 0)),
                  pl.BlockSpec((d, LANE), lambda i: (0, 0))],
        out_specs=pl.BlockSpec((tm, LANE), lambda i: (i, 0)),
        out_shape=jax.ShapeDtypeStruct((nt, LANE), F32),
        compiler_params=_cp(("parallel",)),
        name="router",
    )(x, mod, norm_g, rp)


def _moe_kernel(x_ref, mod_ref, g_ref, dw_ref, w1_ref, w3_ref, w2_ref, o_ref, h_ref, acc_ref):
    e = pl.program_id(1)

    @pl.when(e == 0)
    def _():
        h_ref[...] = _norm2(x_ref, mod_ref, g_ref).astype(h_ref.dtype)
        acc_ref[...] = jnp.zeros_like(acc_ref)

    h = h_ref[...]
    a = jnp.dot(h, w1_ref[0], preferred_element_type=F32)
    b = jnp.dot(h, w3_ref[0], preferred_element_type=F32)
    y = jnp.dot((_silu(a) * b).astype(MXU_DTYPE), w2_ref[0], preferred_element_type=F32)
    dw = dw_ref[...]
    lane = lax.broadcasted_iota(jnp.int32, dw.shape, 1)
    acc_ref[...] += jnp.sum(jnp.where(lane == e, dw, 0.0), axis=-1, keepdims=True) * y

    @pl.when(e == pl.num_programs(1) - 1)
    def _():
        o_ref[...] = x_ref[...] + mod_ref[0][5:6] * acc_ref[...]


def _ffn_moe(x, mod, norm_g, dw, w1, w3, w2, tm, ctx_rows, dec_seq):
    nt, d = x.shape
    ne, _, ff = w1.shape
    return pl.pallas_call(
        _moe_kernel,
        grid=(nt // tm, ne),
        in_specs=[pl.BlockSpec((tm, d), lambda i, e: (i, 0)),
                  pl.BlockSpec((1, 6, d), _mod_row_map(tm, ctx_rows, dec_seq)),
                  pl.BlockSpec((1, d), lambda i, e: (0, 0)),
                  pl.BlockSpec((tm, LANE), lambda i, e: (i, 0)),
                  pl.BlockSpec((1, d, ff), lambda i, e: (e, 0, 0)),
                  pl.BlockSpec((1, d, ff), lambda i, e: (e, 0, 0)),
                  pl.BlockSpec((1, ff, d), lambda i, e: (e, 0, 0))],
        out_specs=pl.BlockSpec((tm, d), lambda i, e: (i, 0)),
        out_shape=jax.ShapeDtypeStruct((nt, d), F32),
        scratch_shapes=[pltpu.VMEM((tm, d), MXU_DTYPE), pltpu.VMEM((tm, d), F32)],
        compiler_params=_cp(("parallel", "arbitrary")),
        name="ffn_moe",
    )(x, mod, norm_g, dw, w1, w3, w2)


def _prep_w_in(w_in):
    depth, d, _ = w_in.shape
    c1 = POOL_WIDTH
    c2 = c1 + 3 * HY_WIDTH
    c3 = c2 + Q_RANK
    c4 = c3 + KV_RANK
    c5 = c4 + ROPE_DIM
    z = lambda n: jnp.zeros((depth, d, n), w_in.dtype)
    cols = [w_in[:, :, c1:c2], w_in[:, :, 0:c1], w_in[:, :, c3:c4],
            z(NOPE_DIM), w_in[:, :, c4:c5], z(HEAD_PAD - QK_DIM),
            z(OFF_Q - OFF_KROPE - HEAD_PAD), w_in[:, :, c2:c3], z(Q_PAD - Q_RANK), w_in[:, :, c5:]]
    out = jnp.concatenate(cols, axis=-1).astype(MXU_DTYPE)
    assert out.shape[-1] == IN_PAD
    return out


def _head_pad_cols(w, width):
    lead = w.shape[:-1]
    w = w.reshape(*lead, N_HEADS, width)
    w = jnp.pad(w, [(0, 0)] * len(lead) + [(0, 0), (0, HEAD_PAD - width)])
    return w.reshape(*lead, N_HEADS * HEAD_PAD)


def _pad_last(a, n):
    return jnp.pad(a, [(0, 0)] * (a.ndim - 1) + [(0, n - a.shape[-1])])


def kernel(x_prompt, x_sample, cache_ckv, cache_krope, c, c_ctx, w_mod, b_mod, norm1_g, norm2_g, w_in, pool_w, pool_scale, pool_out, hy_conv_w, hy_conv_b, hy_w1, hy_b1, hy_w2, hy_b2, hy_w3, hy_freq, hy_decay, hy_bias, hy_out, q_norm_g, w_qb, kv_norm_g, w_kvb, qk_q_g, qk_k_g, mla_out, w_out, ffn_w1, ffn_w3, ffn_w2, moe_router, moe_w1, moe_w3, moe_w2):
    batch, seq, d = x_prompt.shape
    dec_batch, dec_seq, _ = x_sample.shape
    depth = w_mod.shape[0]
    past = cache_ckv.shape[2]
    ctx_rows = batch * seq
    lat_rows = dec_batch * dec_seq
    nt = ctx_rows + lat_rows
    tl = 256
    assert seq % tl == 0 and dec_seq % tl == 0 and ctx_rows % dec_seq == 0 and past % tl == 0
    tm = min(1024, math.gcd(ctx_rows, dec_seq))
    n_ctx_tiles = ctx_rows // tl
    bf = lambda a: a.astype(MXU_DTYPE)

    w_in_p = _prep_w_in(w_in)
    eye = jnp.eye(POOL_GROUPS, dtype=pool_w.dtype)
    gc = POOL_WIDTH // POOL_GROUPS
    pool_bd = bf(jnp.einsum('lgcd,gh->lgchd', pool_w, eye).reshape(depth, POOL_WIDTH, POOL_WIDTH))
    kvb = w_kvb.reshape(depth, KV_RANK, N_HEADS, NOPE_DIM + V_DIM)
    wk = bf(_head_pad_cols(kvb[..., :NOPE_DIM].reshape(depth, KV_RANK, N_HEADS * NOPE_DIM), NOPE_DIM))
    wv = bf(kvb[..., NOPE_DIM:].reshape(depth, KV_RANK, N_HEADS * V_DIM))
    wq = bf(jnp.pad(_head_pad_cols(w_qb, QK_DIM), ((0, 0), (0, Q_PAD - Q_RANK), (0, 0))))
    q_norm_p = _pad_last(q_norm_g, Q_PAD)
    qg = _pad_last(qk_q_g, HEAD_PAD)
    kg = _pad_last(qk_k_g, HEAD_PAD)
    pool_out_b, hy_out_b, mla_out_b, w_out_b = bf(pool_out), bf(hy_out), bf(mla_out), bf(w_out)
    ffn_w1_b, ffn_w3_b, ffn_w2_b = bf(ffn_w1), bf(ffn_w3), bf(ffn_w2)
    moe_w1_b, moe_w3_b, moe_w2_b = bf(moe_w1), bf(moe_w3), bf(moe_w2)

    cos_t, sin_t = _rope_tables(tl, dec_seq)
    n_lat_tiles = dec_seq // tl
    table_map = lambda i: (jnp.where(i < n_ctx_tiles, 0, 1 + (i - n_ctx_tiles) % n_lat_tiles), 0)
    ident_map = lambda i: (0, 0)
    mats_ctx = _dft_matrices(seq)
    mats_lat = _dft_matrices(dec_seq)

    n_cond = 1 + dec_batch
    cond_rows = -(-n_cond // SUBLANE) * SUBLANE
    cond = jnp.concatenate([c_ctx[None, :], c, jnp.zeros((cond_rows - n_cond, d), c.dtype)], axis=0)
    mod_all = _modulation(cond, w_mod, b_mod)[:, :n_cond].reshape(depth, n_cond, 6, d)
    filt_ctx = _hyena_filters(seq, hy_w1, hy_b1, hy_w2, hy_b2, hy_w3, hy_freq, hy_decay)
    filt_lat = _hyena_filters(dec_seq, hy_w1, hy_b1, hy_w2, hy_b2, hy_w3, hy_freq, hy_decay)
    cache_kr_p = jnp.pad(cache_krope, ((0, 0), (0, 0), (0, 0), (NOPE_DIM, HEAD_PAD - QK_DIM)))

    x = jnp.concatenate([x_prompt.reshape(ctx_rows, d), x_sample.reshape(lat_rows, d)], axis=0)
    ckv_list, kr_list = [], []
    for i in range(depth):
        j = i // 2
        mod = mod_all[i]
        r1 = lambda a: a[i].reshape(1, -1)
        proj = _inproj(x, mod, r1(norm1_g), w_in_p[i], tm, ctx_rows, dec_seq)

        vx, vxb, x0, pooled = _local_mixers(proj, hy_conv_w[i], r1(hy_conv_b), pool_bd[i], r1(pool_scale),
                                            tl, n_ctx_tiles, seq, dec_seq)
        bias = r1(hy_bias)
        hy_c = _long_conv(vx[:ctx_rows], vxb[:ctx_rows], x0[:ctx_rows], filt_ctx[i], bias,
                          mats_ctx, batch, seq)
        hy_l = _long_conv(vx[ctx_rows:], vxb[ctx_rows:], x0[ctx_rows:], filt_lat[i], bias,
                          mats_lat, dec_batch, dec_seq)
        hy = jnp.concatenate([hy_c, hy_l], axis=0)

        ckv, k, v = _kv_proj(proj, OFF_KV // KV_RANK, proj, OFF_KROPE // HEAD_PAD, r1(kv_norm_g),
                             wk[i], wv[i], r1(kg), cos_t, sin_t, table_map, tl, True)
        kc, vc = _kv_proj(cache_ckv[:, i].reshape(dec_batch * past, KV_RANK), 0,
                          cache_kr_p[:, i].reshape(dec_batch * past, HEAD_PAD), 0, r1(kv_norm_g),
                          wk[i], wv[i], r1(kg), cos_t, sin_t, ident_map, tl, False)
        q = _q_proj(proj, r1(q_norm_p), wq[i], r1(qg), cos_t, sin_t, table_map, tl)
        att_c = _attention(q, k, v, ctx_rows, 0, batch, seq, min(seq, 256))
        att_l = _attention(q, k, v, lat_rows, ctx_rows, dec_batch, dec_seq, 256, ctx=(kc, vc, past))
        att = jnp.concatenate([att_c, att_l], axis=0)

        x = _merge(pooled, hy, att, proj, x, mod, pool_out_b[i], hy_out_b[i], mla_out_b[i], w_out_b[i],
                   min(tm, 512), ctx_rows, dec_seq)

        if i % 2 == 1:
            dw = _router(x, mod, r1(norm2_g), moe_router[j], tm, ctx_rows, dec_seq)
            x = _ffn_moe(x, mod, r1(norm2_g), dw, moe_w1_b[j], moe_w3_b[j], moe_w2_b[j],
                         min(tm, 512), ctx_rows, dec_seq)
        else:
            x = _ffn_dense(x, mod, r1(norm2_g), ffn_w1_b[j], ffn_w3_b[j], ffn_w2_b[j],
                           tm, ffn_w1.shape[-1] // 2, ctx_rows, dec_seq)

        ckv_list.append(ckv[:ctx_rows].reshape(batch, seq, KV_RANK))
        kr_list.append(proj[:ctx_rows, OFF_KROPE + NOPE_DIM:OFF_KROPE + QK_DIM].reshape(batch, seq, ROPE_DIM))

    y_p = x[:ctx_rows].reshape(batch, seq, d)
    y_s = x[ctx_rows:].reshape(dec_batch, dec_seq, d)
    return (y_p, y_s, jnp.stack(ckv_list, axis=1), jnp.stack(kr_list, axis=1))
```

```python
import functools
import math

import numpy as np
import jax
import jax.numpy as jnp
from jax import lax
from jax.experimental import pallas as pl
from jax.experimental.pallas import tpu as pltpu

F32 = jnp.float32
MXU_DTYPE = jnp.bfloat16
HIGHEST = lax.Precision.HIGHEST

D_MODEL = 1024
GRID_W = 64
EPS = 1e-6
POOL_WIDTH = 256
POOL_GROUPS = 4
POOL_WINDOWS = (2, 4, 8, 16)
HY_WIDTH = 256
HY_SHORT = 3
HY_BANDS = 8
HY_EMB = 1 + 2 * HY_BANDS
HY_FFN = 64
N_HEADS = 8
Q_RANK = 384
KV_RANK = 256
NOPE_DIM = 64
ROPE_DIM = 32
V_DIM = 64
QK_DIM = NOPE_DIM + ROPE_DIM
ROPE_BASE = 10000.0
N_EXPERTS = 8

LANE = 128
SUBLANE = 8
HALO = 16
HEAD_PAD = LANE
Q_PAD = 512
VMEM_LIMIT = 56 * 1024 * 1024
SOFTMAX_C = (QK_DIM ** -0.5) * math.log2(math.e)

OFF_HY = 0
OFF_POOL = 3 * HY_WIDTH
OFF_KV = OFF_POOL + POOL_WIDTH
OFF_KROPE = OFF_KV + KV_RANK
OFF_KROPE2 = OFF_KROPE + HEAD_PAD
OFF_Q = 1536
OFF_G = OFF_Q + Q_PAD
IN_PAD = OFF_G + 3 * D_MODEL


def _cp(sem, vmem=VMEM_LIMIT):
    return pltpu.CompilerParams(dimension_semantics=sem, vmem_limit_bytes=vmem)


def _mm(a, b):
    return jnp.dot(a.astype(MXU_DTYPE), b.astype(MXU_DTYPE), preferred_element_type=F32)


def _mm_f32(a, b):
    return jnp.dot(a, b, preferred_element_type=F32, precision=HIGHEST)


def _sigmoid(x):
    return 1.0 / (1.0 + jnp.exp(-x))


def _silu(x):
    return x * _sigmoid(x)


def _rms(x, n):
    ms = jnp.sum(x * x, axis=-1, keepdims=True) * (1.0 / n)
    return x * lax.rsqrt(ms + EPS)


def _mod_row_map(tm, ctx_rows, dec_seq):
    def index_map(i, *_):
        row0 = i * tm
        return (jnp.where(row0 < ctx_rows, 0, 1 + (row0 - ctx_rows) // dec_seq), 0, 0)
    return index_map


def _mod_kernel(c_ref, w_ref, b_ref, o_ref):
    o_ref[0] = _mm_f32(_silu(c_ref[...]), w_ref[0]) + b_ref[0]


def _modulation(cond, w_mod, b_mod):
    depth, d, n6 = w_mod.shape
    r = cond.shape[0]
    tn = 1024
    return pl.pallas_call(
        _mod_kernel,
        grid=(depth, n6 // tn),
        in_specs=[pl.BlockSpec((r, d), lambda l, j: (0, 0)),
                  pl.BlockSpec((1, d, tn), lambda l, j: (l, 0, j)),
                  pl.BlockSpec((1, 1, tn), lambda l, j: (l, 0, j))],
        out_specs=pl.BlockSpec((1, r, tn), lambda l, j: (l, 0, j)),
        out_shape=jax.ShapeDtypeStruct((depth, r, n6), F32),
        compiler_params=_cp(("parallel", "parallel")),
        name="modulation",
    )(cond, w_mod, b_mod.reshape(depth, 1, n6))


def _inproj_kernel(x_ref, mod_ref, g_ref, w_ref, o_ref, h_ref):
    @pl.when(pl.program_id(1) == 0)
    def _():
        m = mod_ref[0]
        h = _rms(x_ref[...], D_MODEL) * g_ref[...]
        h_ref[...] = (h * (1.0 + m[1:2]) + m[0:1]).astype(h_ref.dtype)
    o_ref[...] = jnp.dot(h_ref[...], w_ref[...], preferred_element_type=F32).astype(o_ref.dtype)


def _inproj(x, mod, norm_g, w, tm, ctx_rows, dec_seq):
    nt, d = x.shape
    n = w.shape[1]
    tn = 1024
    return pl.pallas_call(
        _inproj_kernel,
        grid=(nt // tm, n // tn),
        in_specs=[pl.BlockSpec((tm, d), lambda i, j: (i, 0)),
                  pl.BlockSpec((1, 6, d), _mod_row_map(tm, ctx_rows, dec_seq)),
                  pl.BlockSpec((1, d), lambda i, j: (0, 0)),
                  pl.BlockSpec((d, tn), lambda i, j: (0, j))],
        out_specs=pl.BlockSpec((tm, tn), lambda i, j: (i, j)),
        out_shape=jax.ShapeDtypeStruct((nt, n), MXU_DTYPE),
        scratch_shapes=[pltpu.VMEM((tm, d), MXU_DTYPE)],
        compiler_params=_cp(("parallel", "arbitrary")),
        name="inproj",
    )(x, mod, norm_g, w)


def _local_kernel(u_ref, prev_ref, next_ref, cw_ref, cb_ref, pw_ref, ps_ref,
                  vx_ref, vxb_ref, x0_ref, p_ref, ext_ref, *, tl, n_ctx_tiles, seq, dec_seq):
    i = pl.program_id(0)
    row0 = i * tl
    ctx = i < n_ctx_tiles
    p0 = jnp.where(ctx, row0 % seq, (row0 - n_ctx_tiles * tl) % dec_seq)
    length = jnp.where(ctx, seq, dec_seq)
    first = p0 == 0
    last = p0 + tl == length

    ext_ref[0:SUBLANE, :] = jnp.where(first, 0.0, prev_ref[HALO - SUBLANE:HALO, :].astype(F32))
    ext_ref[SUBLANE:SUBLANE + tl, :] = u_ref[...].astype(F32)
    ext_ref[SUBLANE + tl:, :] = jnp.where(last, 0.0, next_ref[0:SUBLANE, :].astype(F32))

    hw = 3 * HY_WIDTH
    cw = cw_ref[...]
    z = (cb_ref[...]
         + ext_ref[pl.ds(SUBLANE - 1, tl), 0:hw] * cw[0:1]
         + ext_ref[pl.ds(SUBLANE, tl), 0:hw] * cw[1:2]
         + ext_ref[pl.ds(SUBLANE + 1, tl), 0:hw] * cw[2:3])
    x0 = z[:, 0:HY_WIDTH]
    vx = z[:, 2 * HY_WIDTH:3 * HY_WIDTH] * z[:, HY_WIDTH:2 * HY_WIDTH]
    vx_ref[...] = vx
    vxb_ref[...] = vx.astype(vxb_ref.dtype)
    x0_ref[...] = x0

    def e(j):
        return ext_ref[pl.ds(SUBLANE + j, tl), hw:hw + POOL_WIDTH]

    u = e(0)
    s2 = e(-1) + u
    s4 = s2 + e(-2) + e(1)
    s8 = s4 + e(-4) + e(-3) + e(2) + e(3)
    s16 = s8 + e(-8) + e(-7) + e(-6) + e(-5) + e(4) + e(5) + e(6) + e(7)
    lane = lax.broadcasted_iota(jnp.int32, (tl, POOL_WIDTH), 1)
    gc = POOL_WIDTH // POOL_GROUPS
    ssum = jnp.where(lane < gc, s2, jnp.where(lane < 2 * gc, s4, jnp.where(lane < 3 * gc, s8, s16)))
    half = jnp.where(lane < gc, 1, jnp.where(lane < 2 * gc, 2, jnp.where(lane < 3 * gc, 4, 8)))
    t = p0 + lax.broadcasted_iota(jnp.int32, (tl, POOL_WIDTH), 0)
    cnt = jnp.minimum(t + half, length) - jnp.maximum(t - half, 0)
    pooled = ssum / cnt.astype(F32) - u
    p_ref[...] = (_mm(pooled, pw_ref[...]) * ps_ref[...]).astype(p_ref.dtype)


def _local_mixers(proj, conv_w, conv_b, pool_bd, pool_scale, tl, n_ctx_tiles, seq, dec_seq):
    nt = proj.shape[0]
    width = 3 * HY_WIDTH + POOL_WIDTH
    nb = tl // HALO
    last_blk = nt // HALO - 1
    kern = functools.partial(_local_kernel, tl=tl, n_ctx_tiles=n_ctx_tiles, seq=seq, dec_seq=dec_seq)
    row = lambda i: (i, 0)
    const = lambda i: (0, 0)
    return pl.pallas_call(
        kern,
        grid=(nt // tl,),
        in_specs=[pl.BlockSpec((tl, width), row),
                  pl.BlockSpec((HALO, width), lambda i: (jnp.maximum(i * nb - 1, 0), 0)),
                  pl.BlockSpec((HALO, width), lambda i: (jnp.minimum((i + 1) * nb, last_blk), 0)),
                  pl.BlockSpec((HY_SHORT, 3 * HY_WIDTH), const),
                  pl.BlockSpec((1, 3 * HY_WIDTH), const),
                  pl.BlockSpec((POOL_WIDTH, POOL_WIDTH), const),
                  pl.BlockSpec((1, POOL_WIDTH), const)],
        out_specs=[pl.BlockSpec((tl, HY_WIDTH), row)] * 3 + [pl.BlockSpec((tl, POOL_WIDTH), row)],
        out_shape=[jax.ShapeDtypeStruct((nt, HY_WIDTH), F32),
                   jax.ShapeDtypeStruct((nt, HY_WIDTH), MXU_DTYPE),
                   jax.ShapeDtypeStruct((nt, HY_WIDTH), F32),
                   jax.ShapeDtypeStruct((nt, POOL_WIDTH), MXU_DTYPE)],
        scratch_shapes=[pltpu.VMEM((tl + 2 * SUBLANE, width), F32)],
        compiler_params=_cp(("parallel",)),
        name="local_mixers",
    )(proj, proj, proj, conv_w, conv_b, pool_bd, pool_scale)


def _filter_kernel(z_ref, w1_ref, b1_ref, w2_ref, b2_ref, w3_ref, fr_ref, dec_ref, o_ref):
    z = z_ref[...]
    fr = fr_ref[0]
    hdn = jnp.sin(fr * (_mm_f32(z, w1_ref[0]) + b1_ref[0]))
    hdn = jnp.sin(fr * (_mm_f32(hdn, w2_ref[0]) + b2_ref[0]))
    filt = _mm_f32(hdn, w3_ref[0])
    win = jnp.exp(-z[:, 0:1] * jnp.abs(dec_ref[0]))
    o_ref[0] = filt * jnp.concatenate([win, win], axis=-1)


def _hyena_filters(length, w1, b1, w2, b2, w3, freq, decay):
    depth = w1.shape[0]
    t = np.linspace(0.0, 1.0, length, dtype=np.float32)[:, None]
    wpos = (np.float32(2.0 * math.pi / length) * np.arange(length, dtype=np.float32))[:, None]
    bands = np.linspace(1e-4, HY_BANDS - 1, HY_BANDS, dtype=np.float32)[None, :]
    emb_pad = 32
    z = np.zeros((length, emb_pad), np.float32)
    z[:, 0:1] = t
    z[:, 1:1 + HY_BANDS] = np.cos(bands * wpos)
    z[:, 1 + HY_BANDS:HY_EMB] = -np.sin(bands * wpos)
    w1p = jnp.pad(w1, ((0, 0), (0, emb_pad - HY_EMB), (0, 0)))
    tl = min(length, 512)
    lay = lambda l, i: (l, 0, 0)
    r3 = lambda a: a.reshape(depth, 1, a.shape[-1])
    return pl.pallas_call(
        _filter_kernel,
        grid=(depth, length // tl),
        in_specs=[pl.BlockSpec((tl, emb_pad), lambda l, i: (i, 0)),
                  pl.BlockSpec((1, emb_pad, HY_FFN), lay),
                  pl.BlockSpec((1, 1, HY_FFN), lay),
                  pl.BlockSpec((1, HY_FFN, HY_FFN), lay),
                  pl.BlockSpec((1, 1, HY_FFN), lay),
                  pl.BlockSpec((1, HY_FFN, 2 * HY_WIDTH), lay),
                  pl.BlockSpec((1, 1, HY_FFN), lay),
                  pl.BlockSpec((1, 1, HY_WIDTH), lay)],
        out_specs=pl.BlockSpec((1, tl, 2 * HY_WIDTH), lambda l, i: (l, i, 0)),
        out_shape=jax.ShapeDtypeStruct((depth, length, 2 * HY_WIDTH), F32),
        compiler_params=_cp(("parallel", "parallel")),
        name="hyena_filters",
    )(jnp.asarray(z), w1p, r3(b1), w2, r3(b2), w3, r3(freq), r3(decay))


def _dft_matrices(length):
    n = 2 * length
    f = jnp.arange(length, dtype=jnp.int32)[:, None]
    s = jnp.arange(length, dtype=jnp.int32)[None, :]
    ang = ((f * s) % n).astype(F32) * np.float32(2.0 * math.pi / n)
    cos = jnp.cos(ang)
    sin = jnp.sin(ang)
    alt = jnp.where(jnp.arange(length) % 2 == 0, 1.0, -1.0).astype(F32)
    fwd_im = jnp.where(f == 0, alt[None, :], -sin)
    fwd = jnp.concatenate([cos, fwd_im], axis=0)
    inv_re = jnp.where(s == 0, 1.0 / n, (2.0 / n) * cos)
    inv_im = jnp.where(s == 0, alt[:, None] / n, (-2.0 / n) * sin)
    inv = jnp.concatenate([inv_re, inv_im], axis=1)
    return fwd.astype(MXU_DTYPE), inv.astype(MXU_DTYPE)


def _dft_fwd_kernel(g_ref, x_ref, o_ref, *, bg):
    @pl.when(pl.program_id(2) == 0)
    def _():
        o_ref[...] = jnp.zeros_like(o_ref)
    g = g_ref[...]
    for b in range(bg):
        o_ref[b] += jnp.dot(g, x_ref[b], preferred_element_type=F32)


def _dft_fwd(gmat, x, bg):
    bsz, length, cols = x.shape
    tm = min(2 * length, 1024)
    tk = min(length, 512)
    return pl.pallas_call(
        functools.partial(_dft_fwd_kernel, bg=bg),
        grid=(bsz // bg, 2 * length // tm, length // tk),
        in_specs=[pl.BlockSpec((tm, tk), lambda g, m, k: (m, k)),
                  pl.BlockSpec((bg, tk, cols), lambda g, m, k: (g, k, 0))],
        out_specs=pl.BlockSpec((bg, tm, cols), lambda g, m, k: (g, m, 0)),
        out_shape=jax.ShapeDtypeStruct((bsz, 2 * length, cols), F32),
        compiler_params=_cp(("parallel", "parallel", "arbitrary")),
        name="dft_fwd",
    )(gmat, x)


def _dft_inv_kernel(gc_ref, gs_ref, xr_ref, xi_ref, ar_ref, ai_ref, hb0_ref, vx_ref, x0_ref,
                    bias_ref, o_ref, acc_ref, *, bg, tk):
    k = pl.program_id(2)

    @pl.when(k == 0)
    def _():
        acc_ref[...] = jnp.zeros_like(acc_ref)

    c = HY_WIDTH
    ar = ar_ref[...]
    ai = ai_ref[...]
    hb0 = hb0_ref[...]
    f0 = (k * tk + lax.broadcasted_iota(jnp.int32, (tk, c), 0)) == 0
    kr = ar[:, 0:c] + ar[:, c:2 * c] - hb0
    ki = jnp.where(f0, ai[:, 0:c] + ai[:, c:2 * c] - hb0, ai[:, 0:c] - ai[:, c:2 * c])
    gc = gc_ref[...]
    gs = gs_ref[...]
    for b in range(bg):
        xr = xr_ref[b]
        xi = xi_ref[b]
        zr = jnp.where(f0, xr * kr, xr * kr - xi * ki)
        zi = jnp.where(f0, xi * ki, xr * ki + xi * kr)
        acc_ref[b] += (jnp.dot(gc, zr.astype(MXU_DTYPE), preferred_element_type=F32)
                       + jnp.dot(gs, zi.astype(MXU_DTYPE), preferred_element_type=F32))

    @pl.when(k == pl.num_programs(2) - 1)
    def _():
        y = (acc_ref[...] + vx_ref[...] * bias_ref[...]) * x0_ref[...]
        o_ref[...] = y.astype(o_ref.dtype)


def _dft_inv(ginv, xspec, fspec, hb0, vx, x0, bias, bg):
    bsz, length, c = vx.shape
    tm = min(length, 512)
    tk = min(length, 512)
    nk = length // tk
    return pl.pallas_call(
        functools.partial(_dft_inv_kernel, bg=bg, tk=tk),
        grid=(bsz // bg, length // tm, nk),
        in_specs=[pl.BlockSpec((tm, tk), lambda g, m, k: (m, k)),
                  pl.BlockSpec((tm, tk), lambda g, m, k: (m, nk + k)),
                  pl.BlockSpec((bg, tk, c), lambda g, m, k: (g, k, 0)),
                  pl.BlockSpec((bg, tk, c), lambda g, m, k: (g, nk + k, 0)),
                  pl.BlockSpec((tk, 2 * c), lambda g, m, k: (k, 0)),
                  pl.BlockSpec((tk, 2 * c), lambda g, m, k: (nk + k, 0)),
                  pl.BlockSpec((1, c), lambda g, m, k: (0, 0)),
                  pl.BlockSpec((bg, tm, c), lambda g, m, k: (g, m, 0)),
                  pl.BlockSpec((bg, tm, c), lambda g, m, k: (g, m, 0)),
                  pl.BlockSpec((1, c), lambda g, m, k: (0, 0))],
        out_specs=pl.BlockSpec((bg, tm, c), lambda g, m, k: (g, m, 0)),
        out_shape=jax.ShapeDtypeStruct((bsz, length, c), MXU_DTYPE),
        scratch_shapes=[pltpu.VMEM((bg, tm, c), F32)],
        compiler_params=_cp(("parallel", "parallel", "arbitrary")),
        name="dft_inv",
    )(ginv, ginv, xspec, xspec, fspec, fspec, hb0, vx, x0, bias)


def _long_conv(vx, vxb, x0, filt, bias, mats, bsz, length):
    gfwd, ginv = mats
    c = vx.shape[-1]
    bg = math.gcd(bsz, 8)
    fspec = _dft_fwd(gfwd, filt.astype(MXU_DTYPE)[None], 1)[0]
    xspec = _dft_fwd(gfwd, vxb.reshape(bsz, length, c), bg)
    hb0 = filt[0:1, c:2 * c]
    out = _dft_inv(ginv, xspec, fspec, hb0, vx.reshape(bsz, length, c), x0.reshape(bsz, length, c),
                   bias, bg)
    return out.reshape(bsz * length, c)


def _rope_partner(r):
    a = ROPE_DIM // 2
    return a * (r // a) + (r % a + a // 2) % a


def _partner_lanes(a):
    idx = np.arange(a.shape[-1])
    r = idx % HEAD_PAD - NOPE_DIM
    rot = (r >= 0) & (r < ROPE_DIM)
    src = np.where(rot, idx - r + _rope_partner(np.clip(r, 0, ROPE_DIM - 1)), idx)
    return jnp.where(jnp.asarray(rot), a[..., src], jnp.zeros((), a.dtype))


def _head_norm_rope(x, xp, cg, sg, out_scale):
    ms = jnp.sum(x * x, axis=-1, keepdims=True) * (1.0 / QK_DIM)
    return (x * cg + xp * sg) * (lax.rsqrt(ms + EPS) * out_scale)


def _kv_kernel(kva_ref, kr_ref, kr2_ref, ng_ref, wk_ref, wv_ref, vone_ref, kg_ref, kg2_ref,
               cos_ref, sin_ref, ckv_ref, k_ref, v_ref, *, normalize):
    kva = kva_ref[...].astype(F32)
    ckv = _rms(kva, KV_RANK) * ng_ref[...] if normalize else kva
    if normalize:
        ckv_ref[...] = ckv
    cb = ckv.astype(MXU_DTYPE)
    kfull = jnp.dot(cb, wk_ref[...], preferred_element_type=F32)
    kr = kr_ref[...].astype(F32)
    cg = cos_ref[...] * kg_ref[...]
    sg = sin_ref[...] * kg2_ref[...]
    xp = kr2_ref[...].astype(F32)
    for h in range(N_HEADS):
        sl = slice(h * HEAD_PAD, (h + 1) * HEAD_PAD)
        k_ref[:, sl] = _head_norm_rope(kfull[:, sl] + kr, xp, cg, sg, 1.0).astype(k_ref.dtype)
    v = jnp.dot(cb, wv_ref[...], preferred_element_type=F32) + vone_ref[...]
    v_ref[...] = v.astype(v_ref.dtype)


def _kv_proj(src, kv_blk, kr_src, kr_blk, kr2_blk, norm_g, wk, wv, vone, kg, kg2, cos, sin,
             table_map, tl, normalize):
    nt = src.shape[0]
    const = lambda i: (0, 0)
    row = lambda i: (i, 0)
    hw = N_HEADS * HEAD_PAD
    out_shape = [jax.ShapeDtypeStruct((nt, KV_RANK), F32),
                 jax.ShapeDtypeStruct((nt, hw), MXU_DTYPE),
                 jax.ShapeDtypeStruct((nt, hw), MXU_DTYPE)]
    out_specs = [pl.BlockSpec((tl, KV_RANK), row),
                 pl.BlockSpec((tl, hw), row),
                 pl.BlockSpec((tl, hw), row)]
    n_in = 11
    kern = functools.partial(_kv_kernel, normalize=normalize)
    if not normalize:
        out_shape, out_specs = out_shape[1:], out_specs[1:]
        kern = lambda *refs: _kv_kernel(*refs[:n_in], None, *refs[n_in:], normalize=False)
    return pl.pallas_call(
        kern,
        grid=(nt // tl,),
        in_specs=[pl.BlockSpec((tl, KV_RANK), lambda i: (i, kv_blk)),
                  pl.BlockSpec((tl, HEAD_PAD), lambda i: (i, kr_blk)),
                  pl.BlockSpec((tl, HEAD_PAD), lambda i: (i, kr2_blk)),
                  pl.BlockSpec((1, KV_RANK), const),
                  pl.BlockSpec((KV_RANK, hw), const),
                  pl.BlockSpec((KV_RANK, hw), const),
                  pl.BlockSpec((1, hw), const),
                  pl.BlockSpec((1, HEAD_PAD), const),
                  pl.BlockSpec((1, HEAD_PAD), const),
                  pl.BlockSpec((tl, HEAD_PAD), table_map),
                  pl.BlockSpec((tl, HEAD_PAD), table_map)],
        out_specs=out_specs,
        out_shape=out_shape,
        compiler_params=_cp(("parallel",)),
        name="kv_proj" if normalize else "kv_proj_cache",
    )(src, kr_src, kr_src, norm_g, wk, wv, vone, kg, kg2, cos, sin)


def _q_kernel(qa_ref, ng_ref, wq_ref, wq2_ref, qg_ref, qg2_ref, cos_ref, sin_ref, q_ref):
    qn = (_rms(qa_ref[...].astype(F32), Q_RANK) * ng_ref[...]).astype(MXU_DTYPE)
    qfull = jnp.dot(qn, wq_ref[...], preferred_element_type=F32)
    qperm = jnp.dot(qn, wq2_ref[...], preferred_element_type=F32)
    cg = cos_ref[...] * qg_ref[...]
    sg = sin_ref[...] * qg2_ref[...]
    for h in range(N_HEADS):
        sl = slice(h * HEAD_PAD, (h + 1) * HEAD_PAD)
        q_ref[:, sl] = _head_norm_rope(qfull[:, sl], qperm[:, sl], cg, sg, SOFTMAX_C).astype(q_ref.dtype)


def _q_proj(proj, norm_g, wq, wq2, qg, qg2, cos, sin, table_map, tl):
    nt = proj.shape[0]
    const = lambda i: (0, 0)
    hw = N_HEADS * HEAD_PAD
    return pl.pallas_call(
        _q_kernel,
        grid=(nt // tl,),
        in_specs=[pl.BlockSpec((tl, Q_PAD), lambda i: (i, OFF_Q // Q_PAD)),
                  pl.BlockSpec((1, Q_PAD), const),
                  pl.BlockSpec((Q_PAD, hw), const),
                  pl.BlockSpec((Q_PAD, hw), const),
                  pl.BlockSpec((1, HEAD_PAD), const),
                  pl.BlockSpec((1, HEAD_PAD), const),
                  pl.BlockSpec((tl, HEAD_PAD), table_map),
                  pl.BlockSpec((tl, HEAD_PAD), table_map)],
        out_specs=pl.BlockSpec((tl, hw), lambda i: (i, 0)),
        out_shape=jax.ShapeDtypeStruct((nt, hw), MXU_DTYPE),
        compiler_params=_cp(("parallel",)),
        name="q_proj",
    )(proj, norm_g, wq, wq2, qg, qg2, cos, sin)


def _rope_tables(tl, dec_seq):
    a = ROPE_DIM // 2
    t = np.arange(dec_seq)
    pos = np.stack([t // GRID_W, t % GRID_W]).astype(np.float32)
    inv = np.power(np.float32(ROPE_BASE), -np.arange(0, a, 2, dtype=np.float32) / np.float32(a))
    ang = pos[:, :, None] * inv.astype(np.float32)
    cos = np.ones((tl + dec_seq, HEAD_PAD), np.float32)
    sin = np.zeros((tl + dec_seq, HEAD_PAD), np.float32)
    hq = a // 2
    for i in range(2):
        lo = NOPE_DIM + i * a
        cos[tl:, lo:lo + hq] = np.cos(ang[i])
        cos[tl:, lo + hq:lo + a] = np.cos(ang[i])
        sin[tl:, lo:lo + hq] = -np.sin(ang[i])
        sin[tl:, lo + hq:lo + a] = np.sin(ang[i])
    return jnp.asarray(cos), jnp.asarray(sin)


def _attn_kernel(*refs, has_ctx, chunk):
    if has_ctx:
        q_ref, k_ref, v_ref, kc_ref, vc_ref, o_ref = refs
    else:
        q_ref, k_ref, v_ref, o_ref = refs
    nt_dims = (((1,), (1,)), ((), ()))
    tq = q_ref.shape[0]
    length = k_ref.shape[0]
    chunks = [(k_ref, v_ref, j, min(chunk, length - j)) for j in range(0, length, chunk)]
    if has_ctx:
        chunks = [(kc_ref, vc_ref, 0, kc_ref.shape[0])] + chunks
    heads = [slice(h * HEAD_PAD, (h + 1) * HEAD_PAD) for h in range(2)]
    qs = [q_ref[:, hs] for hs in heads]
    m = [jnp.full((tq, 1), -jnp.inf, F32) for _ in heads]
    acc = [jnp.zeros((tq, HEAD_PAD), F32) for _ in heads]
    for kr, vr, start, size in chunks:
        rows = pl.ds(start, size)
        for h, hs in enumerate(heads):
            s = lax.dot_general(qs[h], kr[rows, hs], nt_dims, preferred_element_type=F32)
            m_new = jnp.maximum(m[h], jnp.max(s, axis=-1, keepdims=True))
            p = jnp.exp2(s - m_new).astype(MXU_DTYPE)
            acc[h] = acc[h] * jnp.exp2(m[h] - m_new) + jnp.dot(p, vr[rows, hs],
                                                               preferred_element_type=F32)
            m[h] = m_new
    outs = [a[:, 0:V_DIM] / a[:, V_DIM:V_DIM + 1] for a in acc]
    o_ref[...] = jnp.concatenate(outs, axis=-1).astype(o_ref.dtype)


def _attention(q, k, v, out_rows, row_off, bsz, length, tq, ctx=None):
    assert row_off % length == 0 and length % tq == 0
    qb0 = row_off // tq
    kb0 = row_off // length
    nq = length // tq
    hp = N_HEADS // 2
    in_specs = [pl.BlockSpec((tq, 2 * HEAD_PAD), lambda b, h, i: (qb0 + b * nq + i, h)),
                pl.BlockSpec((length, 2 * HEAD_PAD), lambda b, h, i: (kb0 + b, h)),
                pl.BlockSpec((length, 2 * HEAD_PAD), lambda b, h, i: (kb0 + b, h))]
    args = [q, k, v]
    if ctx is not None:
        kc, vc, past = ctx
        in_specs += [pl.BlockSpec((past, 2 * HEAD_PAD), lambda b, h, i: (b, h)),
                     pl.BlockSpec((past, 2 * HEAD_PAD), lambda b, h, i: (b, h))]
        args += [kc, vc]
    return pl.pallas_call(
        functools.partial(_attn_kernel, has_ctx=ctx is not None, chunk=min(length, 2048)),
        grid=(bsz, hp, nq),
        in_specs=in_specs,
        out_specs=pl.BlockSpec((tq, 2 * V_DIM), lambda b, h, i: (b * nq + i, h)),
        out_shape=jax.ShapeDtypeStruct((out_rows, N_HEADS * V_DIM), MXU_DTYPE),
        compiler_params=_cp(("parallel", "parallel", "arbitrary")),
        name="attention_ctx" if ctx is None else "attention_latent",
    )(*args)


def _merge_kernel(p_ref, h_ref, a_ref, g0_ref, g1_ref, g2_ref, x_ref, mod_ref,
                  wp_ref, wh_ref, wa_ref, wo_ref, o_ref):
    yp = jnp.dot(p_ref[...], wp_ref[...], preferred_element_type=F32)
    yh = jnp.dot(h_ref[...], wh_ref[...], preferred_element_type=F32)
    ya = jnp.dot(a_ref[...], wa_ref[...], preferred_element_type=F32)
    gate = lambda ref: _sigmoid(ref[...].astype(F32))
    merged = gate(g0_ref) * yp + gate(g1_ref) * yh + gate(g2_ref) * ya
    mix = jnp.dot(merged.astype(MXU_DTYPE), wo_ref[...], preferred_element_type=F32)
    o_ref[...] = x_ref[...] + mod_ref[0][2:3] * mix


def _merge(p, hy, att, proj, x, mod, wp, wh, wa, wo, tm, ctx_rows, dec_seq):
    nt, d = x.shape
    row = lambda i: (i, 0)
    const = lambda i: (0, 0)
    g0 = OFF_G // d
    return pl.pallas_call(
        _merge_kernel,
        grid=(nt // tm,),
        in_specs=[pl.BlockSpec((tm, POOL_WIDTH), row),
                  pl.BlockSpec((tm, HY_WIDTH), row),
                  pl.BlockSpec((tm, N_HEADS * V_DIM), row),
                  pl.BlockSpec((tm, d), lambda i: (i, g0)),
                  pl.BlockSpec((tm, d), lambda i: (i, g0 + 1)),
                  pl.BlockSpec((tm, d), lambda i: (i, g0 + 2)),
                  pl.BlockSpec((tm, d), row),
                  pl.BlockSpec((1, 6, d), _mod_row_map(tm, ctx_rows, dec_seq)),
                  pl.BlockSpec((POOL_WIDTH, d), const),
                  pl.BlockSpec((HY_WIDTH, d), const),
                  pl.BlockSpec((N_HEADS * V_DIM, d), const),
                  pl.BlockSpec((d, d), const)],
        out_specs=pl.BlockSpec((tm, d), row),
        out_shape=jax.ShapeDtypeStruct((nt, d), F32),
        compiler_params=_cp(("parallel",)),
        name="merge",
    )(p, hy, att, proj, proj, proj, x, mod, wp, wh, wa, wo)


def _norm2(x_ref, mod_ref, g_ref):
    m = mod_ref[0]
    return (_rms(x_ref[...], D_MODEL) * g_ref[...]) * (1.0 + m[4:5]) + m[3:4]


def _ffn_kernel(x_ref, mod_ref, g_ref, w1_ref, w3_ref, w2_ref, o_ref, h_ref, acc_ref):
    j = pl.program_id(1)

    @pl.when(j == 0)
    def _():
        h_ref[...] = _norm2(x_ref, mod_ref, g_ref).astype(h_ref.dtype)
        acc_ref[...] = jnp.zeros_like(acc_ref)

    h = h_ref[...]
    a = jnp.dot(h, w1_ref[...], preferred_element_type=F32)
    b = jnp.dot(h, w3_ref[...], preferred_element_type=F32)
    acc_ref[...] += jnp.dot((_silu(a) * b).astype(MXU_DTYPE), w2_ref[...], preferred_element_type=F32)

    @pl.when(j == pl.num_programs(1) - 1)
    def _():
        o_ref[...] = x_ref[...] + mod_ref[0][5:6] * acc_ref[...]


def _ffn_dense(x, mod, norm_g, w1, w3, w2, tm, tf, ctx_rows, dec_seq):
    nt, d = x.shape
    ff = w1.shape[1]
    return pl.pallas_call(
        _ffn_kernel,
        grid=(nt // tm, ff // tf),
        in_specs=[pl.BlockSpec((tm, d), lambda i, j: (i, 0)),
                  pl.BlockSpec((1, 6, d), _mod_row_map(tm, ctx_rows, dec_seq)),
                  pl.BlockSpec((1, d), lambda i, j: (0, 0)),
                  pl.BlockSpec((d, tf), lambda i, j: (0, j)),
                  pl.BlockSpec((d, tf), lambda i, j: (0, j)),
                  pl.BlockSpec((tf, d), lambda i, j: (j, 0))],
        out_specs=pl.BlockSpec((tm, d), lambda i, j: (i, 0)),
        out_shape=jax.ShapeDtypeStruct((nt, d), F32),
        scratch_shapes=[pltpu.VMEM((tm, d), MXU_DTYPE), pltpu.VMEM((tm, d), F32)],
        compiler_params=_cp(("parallel", "arbitrary")),
        name="ffn_dense",
    )(x, mod, norm_g, w1, w3, w2)


def _router_kernel(x_ref, mod_ref, g_ref, r_ref, dw_ref):
    h = _norm2(x_ref, mod_ref, g_ref)
    logits = _mm_f32(h, r_ref[...])
    lane = lax.broadcasted_iota(jnp.int32, logits.shape, 1).astype(F32)
    neg = jnp.float32(-jnp.inf)
    logits = jnp.where(lane < N_EXPERTS, logits, neg)
    m1 = jnp.max(logits, axis=-1, keepdims=True)
    i1 = jnp.min(jnp.where(logits == m1, lane, float(LANE)), axis=-1, keepdims=True)
    rest = jnp.where(lane == i1, neg, logits)
    m2 = jnp.max(rest, axis=-1, keepdims=True)
    i2 = jnp.min(jnp.where(rest == m2, lane, float(LANE)), axis=-1, keepdims=True)
    e2 = jnp.exp(m2 - m1)
    w_top = 1.0 / (1.0 + e2)
    dw_ref[...] = jnp.where(lane == i1, w_top, 0.0) + jnp.where(lane == i2, e2 * w_top, 0.0)


def _router(x, mod, norm_g, router, tm, ctx_rows, dec_seq):
    nt, d = x.shape
    rp = jnp.pad(router, ((0, 0), (0, LANE - N_EXPERTS)))
    return pl.pallas_call(
        _router_kernel,
        grid=(nt // tm,),
        in_specs=[pl.BlockSpec((tm, d), lambda i: (i, 0)),
                  pl.BlockSpec((1, 6, d), _mod_row_map(tm, ctx_rows, dec_seq)),
                  pl.BlockSpec((1, d), lambda i: (0, 0)),
                  pl.BlockSpec((d, LANE), lambda i: (0, 0))],
        out_specs=pl.BlockSpec((tm, LANE), lambda i: (i, 0)),
        out_shape=jax.ShapeDtypeStruct((nt, LANE), F32),
        compiler_params=_cp(("parallel",)),
        name="router",
    )(x, mod, norm_g, rp)


def _moe_kernel(x_ref, mod_ref, g_ref, dw_ref, w1_ref, w3_ref, w2_ref, o_ref, h_ref, acc_ref):
    e = pl.program_id(1)

    @pl.when(e == 0)
    def _():
        h_ref[...] = _norm2(x_ref, mod_ref, g_ref).astype(h_ref.dtype)
        acc_ref[...] = jnp.zeros_like(acc_ref)

    h = h_ref[...]
    a = jnp.dot(h, w1_ref[0], preferred_element_type=F32)
    b = jnp.dot(h, w3_ref[0], preferred_element_type=F32)
    y = jnp.dot((_silu(a) * b).astype(MXU_DTYPE), w2_ref[0], preferred_element_type=F32)
    dw = dw_ref[...]
    lane = lax.broadcasted_iota(jnp.int32, dw.shape, 1)
    acc_ref[...] += jnp.sum(jnp.where(lane == e, dw, 0.0), axis=-1, keepdims=True) * y

    @pl.when(e == pl.num_programs(1) - 1)
    def _():
        o_ref[...] = x_ref[...] + mod_ref[0][5:6] * acc_ref[...]


def _ffn_moe(x, mod, norm_g, dw, w1, w3, w2, tm, ctx_rows, dec_seq):
    nt, d = x.shape
    ne, _, ff = w1.shape
    return pl.pallas_call(
        _moe_kernel,
        grid=(nt // tm, ne),
        in_specs=[pl.BlockSpec((tm, d), lambda i, e: (i, 0)),
                  pl.BlockSpec((1, 6, d), _mod_row_map(tm, ctx_rows, dec_seq)),
                  pl.BlockSpec((1, d), lambda i, e: (0, 0)),
                  pl.BlockSpec((tm, LANE), lambda i, e: (i, 0)),
                  pl.BlockSpec((1, d, ff), lambda i, e: (e, 0, 0)),
                  pl.BlockSpec((1, d, ff), lambda i, e: (e, 0, 0)),
                  pl.BlockSpec((1, ff, d), lambda i, e: (e, 0, 0))],
        out_specs=pl.BlockSpec((tm, d), lambda i, e: (i, 0)),
        out_shape=jax.ShapeDtypeStruct((nt, d), F32),
        scratch_shapes=[pltpu.VMEM((tm, d), MXU_DTYPE), pltpu.VMEM((tm, d), F32)],
        compiler_params=_cp(("parallel", "arbitrary")),
        name="ffn_moe",
    )(x, mod, norm_g, dw, w1, w3, w2)


def _prep_w_in(w_in):
    depth, d, _ = w_in.shape
    c1 = POOL_WIDTH
    c2 = c1 + 3 * HY_WIDTH
    c3 = c2 + Q_RANK
    c4 = c3 + KV_RANK
    c5 = c4 + ROPE_DIM
    z = lambda n: jnp.zeros((depth, d, n), w_in.dtype)
    kr_group = jnp.concatenate([z(NOPE_DIM), w_in[:, :, c4:c5], z(HEAD_PAD - QK_DIM)], axis=-1)
    cols = [w_in[:, :, c1:c2], w_in[:, :, 0:c1], w_in[:, :, c3:c4],
            kr_group, _partner_lanes(kr_group),
            w_in[:, :, c2:c3], z(Q_PAD - Q_RANK), w_in[:, :, c5:]]
    out = jnp.concatenate(cols, axis=-1).astype(MXU_DTYPE)
    assert out.shape[-1] == IN_PAD
    return out


def _head_pad_cols(w, width):
    lead = w.shape[:-1]
    w = w.reshape(*lead, N_HEADS, width)
    w = jnp.pad(w, [(0, 0)] * len(lead) + [(0, 0), (0, HEAD_PAD - width)])
    return w.reshape(*lead, N_HEADS * HEAD_PAD)


def _pad_last(a, n):
    return jnp.pad(a, [(0, 0)] * (a.ndim - 1) + [(0, n - a.shape[-1])])


def kernel(x_prompt, x_sample, cache_ckv, cache_krope, c, c_ctx, w_mod, b_mod, norm1_g, norm2_g, w_in, pool_w, pool_scale, pool_out, hy_conv_w, hy_conv_b, hy_w1, hy_b1, hy_w2, hy_b2, hy_w3, hy_freq, hy_decay, hy_bias, hy_out, q_norm_g, w_qb, kv_norm_g, w_kvb, qk_q_g, qk_k_g, mla_out, w_out, ffn_w1, ffn_w3, ffn_w2, moe_router, moe_w1, moe_w3, moe_w2):
    batch, seq, d = x_prompt.shape
    dec_batch, dec_seq, _ = x_sample.shape
    depth = w_mod.shape[0]
    past = cache_ckv.shape[2]
    ctx_rows = batch * seq
    lat_rows = dec_batch * dec_seq
    nt = ctx_rows + lat_rows
    tl = 256
    assert seq % tl == 0 and dec_seq % tl == 0 and ctx_rows % dec_seq == 0 and past % tl == 0
    tm = min(1024, math.gcd(ctx_rows, dec_seq))
    n_ctx_tiles = ctx_rows // tl
    bf = lambda a: a.astype(MXU_DTYPE)

    w_in_p = _prep_w_in(w_in)
    eye = jnp.eye(POOL_GROUPS, dtype=pool_w.dtype)
    gc = POOL_WIDTH // POOL_GROUPS
    pool_bd = bf(jnp.einsum('lgcd,gh->lgchd', pool_w, eye).reshape(depth, POOL_WIDTH, POOL_WIDTH))
    kvb = w_kvb.reshape(depth, KV_RANK, N_HEADS, NOPE_DIM + V_DIM)
    wk = bf(_head_pad_cols(kvb[..., :NOPE_DIM].reshape(depth, KV_RANK, N_HEADS * NOPE_DIM), NOPE_DIM))
    wv = bf(_head_pad_cols(kvb[..., NOPE_DIM:].reshape(depth, KV_RANK, N_HEADS * V_DIM), V_DIM))
    vone = jnp.tile((jnp.arange(HEAD_PAD) == V_DIM).astype(F32), N_HEADS)[None, :]
    wq_f = jnp.pad(_head_pad_cols(w_qb, QK_DIM), ((0, 0), (0, Q_PAD - Q_RANK), (0, 0)))
    wq, wq2 = bf(wq_f), bf(_partner_lanes(wq_f))
    q_norm_p = _pad_last(q_norm_g, Q_PAD)
    qg = _pad_last(qk_q_g, HEAD_PAD)
    kg = _pad_last(qk_k_g, HEAD_PAD)
    qg2, kg2 = _partner_lanes(qg), _partner_lanes(kg)
    pool_out_b, hy_out_b, mla_out_b, w_out_b = bf(pool_out), bf(hy_out), bf(mla_out), bf(w_out)
    ffn_w1_b, ffn_w3_b, ffn_w2_b = bf(ffn_w1), bf(ffn_w3), bf(ffn_w2)
    moe_w1_b, moe_w3_b, moe_w2_b = bf(moe_w1), bf(moe_w3), bf(moe_w2)

    cos_t, sin_t = _rope_tables(tl, dec_seq)
    n_lat_tiles = dec_seq // tl
    table_map = lambda i: (jnp.where(i < n_ctx_tiles, 0, 1 + (i - n_ctx_tiles) % n_lat_tiles), 0)
    ident_map = lambda i: (0, 0)
    mats_ctx = _dft_matrices(seq)
    mats_lat = _dft_matrices(dec_seq)

    n_cond = 1 + dec_batch
    cond_rows = -(-n_cond // SUBLANE) * SUBLANE
    cond = jnp.concatenate([c_ctx[None, :], c, jnp.zeros((cond_rows - n_cond, d), c.dtype)], axis=0)
    mod_all = _modulation(cond, w_mod, b_mod)[:, :n_cond].reshape(depth, n_cond, 6, d)
    filt_ctx = _hyena_filters(seq, hy_w1, hy_b1, hy_w2, hy_b2, hy_w3, hy_freq, hy_decay)
    filt_lat = _hyena_filters(dec_seq, hy_w1, hy_b1, hy_w2, hy_b2, hy_w3, hy_freq, hy_decay)
    cache_kr_p = jnp.pad(cache_krope, ((0, 0), (0, 0), (0, 0), (NOPE_DIM, HEAD_PAD - QK_DIM)))
    cache_kr_p = jnp.concatenate([cache_kr_p, _partner_lanes(cache_kr_p)], axis=-1)

    x = jnp.concatenate([x_prompt.reshape(ctx_rows, d), x_sample.reshape(lat_rows, d)], axis=0)
    ckv_list, kr_list = [], []
    for i in range(depth):
        j = i // 2
        mod = mod_all[i]
        r1 = lambda a: a[i].reshape(1, -1)
        proj = _inproj(x, mod, r1(norm1_g), w_in_p[i], tm, ctx_rows, dec_seq)

        vx, vxb, x0, pooled = _local_mixers(proj, hy_conv_w[i], r1(hy_conv_b), pool_bd[i], r1(pool_scale),
                                            tl, n_ctx_tiles, seq, dec_seq)
        bias = r1(hy_bias)
        hy_c = _long_conv(vx[:ctx_rows], vxb[:ctx_rows], x0[:ctx_rows], filt_ctx[i], bias,
                          mats_ctx, batch, seq)
        hy_l = _long_conv(vx[ctx_rows:], vxb[ctx_rows:], x0[ctx_rows:], filt_lat[i], bias,
                          mats_lat, dec_batch, dec_seq)
        hy = jnp.concatenate([hy_c, hy_l], axis=0)

        ckv, k, v = _kv_proj(proj, OFF_KV // KV_RANK, proj, OFF_KROPE // HEAD_PAD, OFF_KROPE2 // HEAD_PAD,
                             r1(kv_norm_g), wk[i], wv[i], vone, r1(kg), r1(kg2), cos_t, sin_t,
                             table_map, tl, True)
        kc, vc = _kv_proj(cache_ckv[:, i].reshape(dec_batch * past, KV_RANK), 0,
                          cache_kr_p[:, i].reshape(dec_batch * past, 2 * HEAD_PAD), 0, 1,
                          r1(kv_norm_g), wk[i], wv[i], vone, r1(kg), r1(kg2), cos_t, sin_t,
                          ident_map, tl, False)
        q = _q_proj(proj, r1(q_norm_p), wq[i], wq2[i], r1(qg), r1(qg2), cos_t, sin_t, table_map, tl)
        att_c = _attention(q, k, v, ctx_rows, 0, batch, seq, min(seq, 256))
        att_l = _attention(q, k, v, lat_rows, ctx_rows, dec_batch, dec_seq, min(dec_seq, 512),
                           ctx=(kc, vc, past))
        att = jnp.concatenate([att_c, att_l], axis=0)

        x = _merge(pooled, hy, att, proj, x, mod, pool_out_b[i], hy_out_b[i], mla_out_b[i], w_out_b[i],
                   min(tm, 512), ctx_rows, dec_seq)

        if i % 2 == 1:
            dw = _router(x, mod, r1(norm2_g), moe_router[j], tm, ctx_rows, dec_seq)
            x = _ffn_moe(x, mod, r1(norm2_g), dw, moe_w1_b[j], moe_w3_b[j], moe_w2_b[j],
                         min(tm, 512), ctx_rows, dec_seq)
        else:
            x = _ffn_dense(x, mod, r1(norm2_g), ffn_w1_b[j], ffn_w3_b[j], ffn_w2_b[j],
                           tm, ffn_w1.shape[-1] // 2, ctx_rows, dec_seq)

        ckv_list.append(ckv[:ctx_rows].reshape(batch, seq, KV_RANK))
        kr_list.append(proj[:ctx_rows, OFF_KROPE + NOPE_DIM:OFF_KROPE + QK_DIM].astype(F32)
                       .reshape(batch, seq, ROPE_DIM))

    y_p = x[:ctx_rows].reshape(batch, seq, d)
    y_s = x[ctx_rows:].reshape(dec_batch, dec_seq, d)
    return (y_p, y_s, jnp.stack(ckv_list, axis=1), jnp.stack(kr_list, axis=1))
```

```python
import functools
import math

import numpy as np
import jax
import jax.numpy as jnp
from jax import lax
from jax.experimental import pallas as pl
from jax.experimental.pallas import tpu as pltpu

F32 = jnp.float32
MXU_DTYPE = jnp.bfloat16
HIGHEST = lax.Precision.HIGHEST

D_MODEL = 1024
GRID_W = 64
EPS = 1e-6
POOL_WIDTH = 256
POOL_GROUPS = 4
POOL_WINDOWS = (2, 4, 8, 16)
HY_WIDTH = 256
HY_SHORT = 3
HY_BANDS = 8
HY_EMB = 1 + 2 * HY_BANDS
HY_FFN = 64
N_HEADS = 8
Q_RANK = 384
KV_RANK = 256
NOPE_DIM = 64
ROPE_DIM = 32
V_DIM = 64
QK_DIM = NOPE_DIM + ROPE_DIM
ROPE_BASE = 10000.0
N_EXPERTS = 8

LANE = 128
SUBLANE = 8
HALO = 16
HEAD_PAD = LANE
Q_PAD = 512
MOE_ROWS = 512
VMEM_LIMIT = 56 * 1024 * 1024
SOFTMAX_C = (QK_DIM ** -0.5) * math.log2(math.e)

OFF_HY = 0
OFF_POOL = 3 * HY_WIDTH
OFF_KV = OFF_POOL + POOL_WIDTH
OFF_KROPE = OFF_KV + KV_RANK
OFF_KROPE2 = OFF_KROPE + HEAD_PAD
OFF_Q = 1536
OFF_G = OFF_Q + Q_PAD
IN_PAD = OFF_G + 3 * D_MODEL


def _cp(sem, vmem=VMEM_LIMIT):
    return pltpu.CompilerParams(dimension_semantics=sem, vmem_limit_bytes=vmem)


def _mm(a, b):
    return jnp.dot(a.astype(MXU_DTYPE), b.astype(MXU_DTYPE), preferred_element_type=F32)


def _mm_f32(a, b):
    return jnp.dot(a, b, preferred_element_type=F32, precision=HIGHEST)


def _sigmoid(x):
    return 1.0 / (1.0 + jnp.exp(-x))


def _silu(x):
    return x * _sigmoid(x)


def _rms(x, n):
    ms = jnp.sum(x * x, axis=-1, keepdims=True) * (1.0 / n)
    return x * lax.rsqrt(ms + EPS)


def _mod_row_map(tm, ctx_rows, dec_seq):
    def index_map(i, *_):
        row0 = i * tm
        return (jnp.where(row0 < ctx_rows, 0, 1 + (row0 - ctx_rows) // dec_seq), 0, 0)
    return index_map


def _mod_kernel(c_ref, w_ref, b_ref, o_ref):
    o_ref[0] = _mm_f32(_silu(c_ref[...]), w_ref[0]) + b_ref[0]


def _modulation(cond, w_mod, b_mod):
    depth, d, n6 = w_mod.shape
    r = cond.shape[0]
    tn = 1024
    return pl.pallas_call(
        _mod_kernel,
        grid=(depth, n6 // tn),
        in_specs=[pl.BlockSpec((r, d), lambda l, j: (0, 0)),
                  pl.BlockSpec((1, d, tn), lambda l, j: (l, 0, j)),
                  pl.BlockSpec((1, 1, tn), lambda l, j: (l, 0, j))],
        out_specs=pl.BlockSpec((1, r, tn), lambda l, j: (l, 0, j)),
        out_shape=jax.ShapeDtypeStruct((depth, r, n6), F32),
        compiler_params=_cp(("parallel", "parallel")),
        name="modulation",
    )(cond, w_mod, b_mod.reshape(depth, 1, n6))


def _inproj_kernel(x_ref, mod_ref, g_ref, w_ref, o_ref, h_ref):
    @pl.when(pl.program_id(1) == 0)
    def _():
        m = mod_ref[0]
        h = _rms(x_ref[...], D_MODEL) * g_ref[...]
        h_ref[...] = (h * (1.0 + m[1:2]) + m[0:1]).astype(h_ref.dtype)
    o_ref[...] = jnp.dot(h_ref[...], w_ref[...], preferred_element_type=F32).astype(o_ref.dtype)


def _inproj(x, mod, norm_g, w, tm, ctx_rows, dec_seq):
    nt, d = x.shape
    n = w.shape[1]
    tn = 1024
    return pl.pallas_call(
        _inproj_kernel,
        grid=(nt // tm, n // tn),
        in_specs=[pl.BlockSpec((tm, d), lambda i, j: (i, 0)),
                  pl.BlockSpec((1, 6, d), _mod_row_map(tm, ctx_rows, dec_seq)),
                  pl.BlockSpec((1, d), lambda i, j: (0, 0)),
                  pl.BlockSpec((d, tn), lambda i, j: (0, j))],
        out_specs=pl.BlockSpec((tm, tn), lambda i, j: (i, j)),
        out_shape=jax.ShapeDtypeStruct((nt, n), MXU_DTYPE),
        scratch_shapes=[pltpu.VMEM((tm, d), MXU_DTYPE)],
        compiler_params=_cp(("parallel", "arbitrary")),
        name="inproj",
    )(x, mod, norm_g, w)


def _local_kernel(u_ref, prev_ref, next_ref, cw_ref, cb_ref, pw_ref, ps_ref,
                  vx_ref, vxb_ref, x0_ref, p_ref, ext_ref, *, tl, n_ctx_tiles, seq, dec_seq):
    i = pl.program_id(0)
    row0 = i * tl
    ctx = i < n_ctx_tiles
    p0 = jnp.where(ctx, row0 % seq, (row0 - n_ctx_tiles * tl) % dec_seq)
    length = jnp.where(ctx, seq, dec_seq)
    first = p0 == 0
    last = p0 + tl == length

    ext_ref[0:SUBLANE, :] = jnp.where(first, 0.0, prev_ref[HALO - SUBLANE:HALO, :].astype(F32))
    ext_ref[SUBLANE:SUBLANE + tl, :] = u_ref[...].astype(F32)
    ext_ref[SUBLANE + tl:, :] = jnp.where(last, 0.0, next_ref[0:SUBLANE, :].astype(F32))

    hw = 3 * HY_WIDTH
    cw = cw_ref[...]
    z = (cb_ref[...]
         + ext_ref[pl.ds(SUBLANE - 1, tl), 0:hw] * cw[0:1]
         + ext_ref[pl.ds(SUBLANE, tl), 0:hw] * cw[1:2]
         + ext_ref[pl.ds(SUBLANE + 1, tl), 0:hw] * cw[2:3])
    x0 = z[:, 0:HY_WIDTH]
    vx = z[:, 2 * HY_WIDTH:3 * HY_WIDTH] * z[:, HY_WIDTH:2 * HY_WIDTH]
    vx_ref[...] = vx
    vxb_ref[...] = vx.astype(vxb_ref.dtype)
    x0_ref[...] = x0

    def e(j):
        return ext_ref[pl.ds(SUBLANE + j, tl), hw:hw + POOL_WIDTH]

    u = e(0)
    s2 = e(-1) + u
    s4 = s2 + e(-2) + e(1)
    s8 = s4 + e(-4) + e(-3) + e(2) + e(3)
    s16 = s8 + e(-8) + e(-7) + e(-6) + e(-5) + e(4) + e(5) + e(6) + e(7)
    lane = lax.broadcasted_iota(jnp.int32, (tl, POOL_WIDTH), 1)
    gc = POOL_WIDTH // POOL_GROUPS
    ssum = jnp.where(lane < gc, s2, jnp.where(lane < 2 * gc, s4, jnp.where(lane < 3 * gc, s8, s16)))
    half = jnp.where(lane < gc, 1, jnp.where(lane < 2 * gc, 2, jnp.where(lane < 3 * gc, 4, 8)))
    t = p0 + lax.broadcasted_iota(jnp.int32, (tl, POOL_WIDTH), 0)
    cnt = jnp.minimum(t + half, length) - jnp.maximum(t - half, 0)
    pooled = ssum / cnt.astype(F32) - u
    p_ref[...] = (_mm(pooled, pw_ref[...]) * ps_ref[...]).astype(p_ref.dtype)


def _local_mixers(proj, conv_w, conv_b, pool_bd, pool_scale, tl, n_ctx_tiles, seq, dec_seq):
    nt = proj.shape[0]
    width = 3 * HY_WIDTH + POOL_WIDTH
    nb = tl // HALO
    last_blk = nt // HALO - 1
    kern = functools.partial(_local_kernel, tl=tl, n_ctx_tiles=n_ctx_tiles, seq=seq, dec_seq=dec_seq)
    row = lambda i: (i, 0)
    const = lambda i: (0, 0)
    return pl.pallas_call(
        kern,
        grid=(nt // tl,),
        in_specs=[pl.BlockSpec((tl, width), row),
                  pl.BlockSpec((HALO, width), lambda i: (jnp.maximum(i * nb - 1, 0), 0)),
                  pl.BlockSpec((HALO, width), lambda i: (jnp.minimum((i + 1) * nb, last_blk), 0)),
                  pl.BlockSpec((HY_SHORT, 3 * HY_WIDTH), const),
                  pl.BlockSpec((1, 3 * HY_WIDTH), const),
                  pl.BlockSpec((POOL_WIDTH, POOL_WIDTH), const),
                  pl.BlockSpec((1, POOL_WIDTH), const)],
        out_specs=[pl.BlockSpec((tl, HY_WIDTH), row)] * 3 + [pl.BlockSpec((tl, POOL_WIDTH), row)],
        out_shape=[jax.ShapeDtypeStruct((nt, HY_WIDTH), F32),
                   jax.ShapeDtypeStruct((nt, HY_WIDTH), MXU_DTYPE),
                   jax.ShapeDtypeStruct((nt, HY_WIDTH), F32),
                   jax.ShapeDtypeStruct((nt, POOL_WIDTH), MXU_DTYPE)],
        scratch_shapes=[pltpu.VMEM((tl + 2 * SUBLANE, width), F32)],
        compiler_params=_cp(("parallel",)),
        name="local_mixers",
    )(proj, proj, proj, conv_w, conv_b, pool_bd, pool_scale)


def _filter_kernel(z_ref, w1_ref, b1_ref, w2_ref, b2_ref, w3_ref, fr_ref, dec_ref, o_ref):
    z = z_ref[...]
    fr = fr_ref[0]
    hdn = jnp.sin(fr * (_mm_f32(z, w1_ref[0]) + b1_ref[0]))
    hdn = jnp.sin(fr * (_mm_f32(hdn, w2_ref[0]) + b2_ref[0]))
    filt = _mm_f32(hdn, w3_ref[0])
    win = jnp.exp(-z[:, 0:1] * jnp.abs(dec_ref[0]))
    o_ref[0] = filt * jnp.concatenate([win, win], axis=-1)


def _hyena_filters(length, w1, b1, w2, b2, w3, freq, decay):
    depth = w1.shape[0]
    t = np.linspace(0.0, 1.0, length, dtype=np.float32)[:, None]
    wpos = (np.float32(2.0 * math.pi / length) * np.arange(length, dtype=np.float32))[:, None]
    bands = np.linspace(1e-4, HY_BANDS - 1, HY_BANDS, dtype=np.float32)[None, :]
    emb_pad = 32
    z = np.zeros((length, emb_pad), np.float32)
    z[:, 0:1] = t
    z[:, 1:1 + HY_BANDS] = np.cos(bands * wpos)
    z[:, 1 + HY_BANDS:HY_EMB] = -np.sin(bands * wpos)
    w1p = jnp.pad(w1, ((0, 0), (0, emb_pad - HY_EMB), (0, 0)))
    tl = min(length, 512)
    lay = lambda l, i: (l, 0, 0)
    r3 = lambda a: a.reshape(depth, 1, a.shape[-1])
    return pl.pallas_call(
        _filter_kernel,
        grid=(depth, length // tl),
        in_specs=[pl.BlockSpec((tl, emb_pad), lambda l, i: (i, 0)),
                  pl.BlockSpec((1, emb_pad, HY_FFN), lay),
                  pl.BlockSpec((1, 1, HY_FFN), lay),
                  pl.BlockSpec((1, HY_FFN, HY_FFN), lay),
                  pl.BlockSpec((1, 1, HY_FFN), lay),
                  pl.BlockSpec((1, HY_FFN, 2 * HY_WIDTH), lay),
                  pl.BlockSpec((1, 1, HY_FFN), lay),
                  pl.BlockSpec((1, 1, HY_WIDTH), lay)],
        out_specs=pl.BlockSpec((1, tl, 2 * HY_WIDTH), lambda l, i: (l, i, 0)),
        out_shape=jax.ShapeDtypeStruct((depth, length, 2 * HY_WIDTH), F32),
        compiler_params=_cp(("parallel", "parallel")),
        name="hyena_filters",
    )(jnp.asarray(z), w1p, r3(b1), w2, r3(b2), w3, r3(freq), r3(decay))


def _dft_matrices(length):
    n = 2 * length
    f = jnp.arange(length, dtype=jnp.int32)[:, None]
    s = jnp.arange(length, dtype=jnp.int32)[None, :]
    ang = ((f * s) % n).astype(F32) * np.float32(2.0 * math.pi / n)
    cos = jnp.cos(ang)
    sin = jnp.sin(ang)
    alt = jnp.where(jnp.arange(length) % 2 == 0, 1.0, -1.0).astype(F32)
    fwd_im = jnp.where(f == 0, alt[None, :], -sin)
    fwd = jnp.concatenate([cos, fwd_im], axis=0)
    inv_re = jnp.where(s == 0, 1.0 / n, (2.0 / n) * cos)
    inv_im = jnp.where(s == 0, alt[:, None] / n, (-2.0 / n) * sin)
    inv = jnp.concatenate([inv_re, inv_im], axis=1)
    return fwd.astype(MXU_DTYPE), inv.astype(MXU_DTYPE)


def _dft_fwd_kernel(g_ref, x_ref, o_ref, *, bg):
    @pl.when(pl.program_id(2) == 0)
    def _():
        o_ref[...] = jnp.zeros_like(o_ref)
    g = g_ref[...]
    for b in range(bg):
        o_ref[b] += jnp.dot(g, x_ref[b], preferred_element_type=F32)


def _dft_fwd(gmat, x, bg):
    bsz, length, cols = x.shape
    tm = min(2 * length, 1024)
    tk = min(length, 512)
    return pl.pallas_call(
        functools.partial(_dft_fwd_kernel, bg=bg),
        grid=(bsz // bg, 2 * length // tm, length // tk),
        in_specs=[pl.BlockSpec((tm, tk), lambda g, m, k: (m, k)),
                  pl.BlockSpec((bg, tk, cols), lambda g, m, k: (g, k, 0))],
        out_specs=pl.BlockSpec((bg, tm, cols), lambda g, m, k: (g, m, 0)),
        out_shape=jax.ShapeDtypeStruct((bsz, 2 * length, cols), F32),
        compiler_params=_cp(("parallel", "parallel", "arbitrary")),
        name="dft_fwd",
    )(gmat, x)


def _dft_inv_kernel(gc_ref, gs_ref, xr_ref, xi_ref, ar_ref, ai_ref, hb0_ref, vx_ref, x0_ref,
                    bias_ref, o_ref, acc_ref, *, bg, tk):
    k = pl.program_id(2)

    @pl.when(k == 0)
    def _():
        acc_ref[...] = jnp.zeros_like(acc_ref)

    c = HY_WIDTH
    ar = ar_ref[...]
    ai = ai_ref[...]
    hb0 = hb0_ref[...]
    f0 = (k * tk + lax.broadcasted_iota(jnp.int32, (tk, c), 0)) == 0
    kr = ar[:, 0:c] + ar[:, c:2 * c] - hb0
    ki = jnp.where(f0, ai[:, 0:c] + ai[:, c:2 * c] - hb0, ai[:, 0:c] - ai[:, c:2 * c])
    gc = gc_ref[...]
    gs = gs_ref[...]
    for b in range(bg):
        xr = xr_ref[b]
        xi = xi_ref[b]
        zr = jnp.where(f0, xr * kr, xr * kr - xi * ki)
        zi = jnp.where(f0, xi * ki, xr * ki + xi * kr)
        acc_ref[b] += (jnp.dot(gc, zr.astype(MXU_DTYPE), preferred_element_type=F32)
                       + jnp.dot(gs, zi.astype(MXU_DTYPE), preferred_element_type=F32))

    @pl.when(k == pl.num_programs(2) - 1)
    def _():
        y = (acc_ref[...] + vx_ref[...] * bias_ref[...]) * x0_ref[...]
        o_ref[...] = y.astype(o_ref.dtype)


def _dft_inv(ginv, xspec, fspec, hb0, vx, x0, bias, bg):
    bsz, length, c = vx.shape
    tm = min(length, 512)
    tk = min(length, 512)
    nk = length // tk
    return pl.pallas_call(
        functools.partial(_dft_inv_kernel, bg=bg, tk=tk),
        grid=(bsz // bg, length // tm, nk),
        in_specs=[pl.BlockSpec((tm, tk), lambda g, m, k: (m, k)),
                  pl.BlockSpec((tm, tk), lambda g, m, k: (m, nk + k)),
                  pl.BlockSpec((bg, tk, c), lambda g, m, k: (g, k, 0)),
                  pl.BlockSpec((bg, tk, c), lambda g, m, k: (g, nk + k, 0)),
                  pl.BlockSpec((tk, 2 * c), lambda g, m, k: (k, 0)),
                  pl.BlockSpec((tk, 2 * c), lambda g, m, k: (nk + k, 0)),
                  pl.BlockSpec((1, c), lambda g, m, k: (0, 0)),
                  pl.BlockSpec((bg, tm, c), lambda g, m, k: (g, m, 0)),
                  pl.BlockSpec((bg, tm, c), lambda g, m, k: (g, m, 0)),
                  pl.BlockSpec((1, c), lambda g, m, k: (0, 0))],
        out_specs=pl.BlockSpec((bg, tm, c), lambda g, m, k: (g, m, 0)),
        out_shape=jax.ShapeDtypeStruct((bsz, length, c), MXU_DTYPE),
        scratch_shapes=[pltpu.VMEM((bg, tm, c), F32)],
        compiler_params=_cp(("parallel", "parallel", "arbitrary")),
        name="dft_inv",
    )(ginv, ginv, xspec, xspec, fspec, fspec, hb0, vx, x0, bias)


def _long_conv(vx, vxb, x0, filt, bias, mats, bsz, length):
    gfwd, ginv = mats
    c = vx.shape[-1]
    bg = math.gcd(bsz, 8)
    fspec = _dft_fwd(gfwd, filt.astype(MXU_DTYPE)[None], 1)[0]
    xspec = _dft_fwd(gfwd, vxb.reshape(bsz, length, c), bg)
    hb0 = filt[0:1, c:2 * c]
    out = _dft_inv(ginv, xspec, fspec, hb0, vx.reshape(bsz, length, c), x0.reshape(bsz, length, c),
                   bias, bg)
    return out.reshape(bsz * length, c)


def _rope_partner(r):
    a = ROPE_DIM // 2
    return a * (r // a) + (r % a + a // 2) % a


def _partner_lanes(a):
    idx = np.arange(a.shape[-1])
    r = idx % HEAD_PAD - NOPE_DIM
    rot = (r >= 0) & (r < ROPE_DIM)
    src = np.where(rot, idx - r + _rope_partner(np.clip(r, 0, ROPE_DIM - 1)), idx)
    return jnp.where(jnp.asarray(rot), a[..., src], jnp.zeros((), a.dtype))


def _head_norm_rope(x, xp, cg, sg, out_scale):
    ms = jnp.sum(x * x, axis=-1, keepdims=True) * (1.0 / QK_DIM)
    return (x * cg + xp * sg) * (lax.rsqrt(ms + EPS) * out_scale)


def _kv_kernel(kva_ref, kr_ref, kr2_ref, ng_ref, wk_ref, wv_ref, vone_ref, kg_ref, kg2_ref,
               cos_ref, sin_ref, ckv_ref, k_ref, v_ref, *, normalize):
    kva = kva_ref[...].astype(F32)
    ckv = _rms(kva, KV_RANK) * ng_ref[...] if normalize else kva
    if normalize:
        ckv_ref[...] = ckv
    cb = ckv.astype(MXU_DTYPE)
    kfull = jnp.dot(cb, wk_ref[...], preferred_element_type=F32)
    kr = kr_ref[...].astype(F32)
    cg = cos_ref[...] * kg_ref[...]
    sg = sin_ref[...] * kg2_ref[...]
    xp = kr2_ref[...].astype(F32)
    for h in range(N_HEADS):
        sl = slice(h * HEAD_PAD, (h + 1) * HEAD_PAD)
        k_ref[:, sl] = _head_norm_rope(kfull[:, sl] + kr, xp, cg, sg, 1.0).astype(k_ref.dtype)
    v = jnp.dot(cb, wv_ref[...], preferred_element_type=F32) + vone_ref[...]
    v_ref[...] = v.astype(v_ref.dtype)


def _kv_proj(src, kv_blk, kr_src, kr_blk, kr2_blk, norm_g, wk, wv, vone, kg, kg2, cos, sin,
             table_map, tl, normalize):
    nt = src.shape[0]
    const = lambda i: (0, 0)
    row = lambda i: (i, 0)
    hw = N_HEADS * HEAD_PAD
    out_shape = [jax.ShapeDtypeStruct((nt, KV_RANK), F32),
                 jax.ShapeDtypeStruct((nt, hw), MXU_DTYPE),
                 jax.ShapeDtypeStruct((nt, hw), MXU_DTYPE)]
    out_specs = [pl.BlockSpec((tl, KV_RANK), row),
                 pl.BlockSpec((tl, hw), row),
                 pl.BlockSpec((tl, hw), row)]
    n_in = 11
    kern = functools.partial(_kv_kernel, normalize=normalize)
    if not normalize:
        out_shape, out_specs = out_shape[1:], out_specs[1:]
        kern = lambda *refs: _kv_kernel(*refs[:n_in], None, *refs[n_in:], normalize=False)
    return pl.pallas_call(
        kern,
        grid=(nt // tl,),
        in_specs=[pl.BlockSpec((tl, KV_RANK), lambda i: (i, kv_blk)),
                  pl.BlockSpec((tl, HEAD_PAD), lambda i: (i, kr_blk)),
                  pl.BlockSpec((tl, HEAD_PAD), lambda i: (i, kr2_blk)),
                  pl.BlockSpec((1, KV_RANK), const),
                  pl.BlockSpec((KV_RANK, hw), const),
                  pl.BlockSpec((KV_RANK, hw), const),
                  pl.BlockSpec((1, hw), const),
                  pl.BlockSpec((1, HEAD_PAD), const),
                  pl.BlockSpec((1, HEAD_PAD), const),
                  pl.BlockSpec((tl, HEAD_PAD), table_map),
                  pl.BlockSpec((tl, HEAD_PAD), table_map)],
        out_specs=out_specs,
        out_shape=out_shape,
        compiler_params=_cp(("parallel",)),
        name="kv_proj" if normalize else "kv_proj_cache",
    )(src, kr_src, kr_src, norm_g, wk, wv, vone, kg, kg2, cos, sin)


def _q_kernel(qa_ref, ng_ref, wq_ref, wq2_ref, qg_ref, qg2_ref, cos_ref, sin_ref, q_ref):
    qn = (_rms(qa_ref[...].astype(F32), Q_RANK) * ng_ref[...]).astype(MXU_DTYPE)
    qfull = jnp.dot(qn, wq_ref[...], preferred_element_type=F32)
    qperm = jnp.dot(qn, wq2_ref[...], preferred_element_type=F32)
    cg = cos_ref[...] * qg_ref[...]
    sg = sin_ref[...] * qg2_ref[...]
    for h in range(N_HEADS):
        sl = slice(h * HEAD_PAD, (h + 1) * HEAD_PAD)
        q_ref[:, sl] = _head_norm_rope(qfull[:, sl], qperm[:, sl], cg, sg, SOFTMAX_C).astype(q_ref.dtype)


def _q_proj(proj, norm_g, wq, wq2, qg, qg2, cos, sin, table_map, tl):
    nt = proj.shape[0]
    const = lambda i: (0, 0)
    hw = N_HEADS * HEAD_PAD
    return pl.pallas_call(
        _q_kernel,
        grid=(nt // tl,),
        in_specs=[pl.BlockSpec((tl, Q_PAD), lambda i: (i, OFF_Q // Q_PAD)),
                  pl.BlockSpec((1, Q_PAD), const),
                  pl.BlockSpec((Q_PAD, hw), const),
                  pl.BlockSpec((Q_PAD, hw), const),
                  pl.BlockSpec((1, HEAD_PAD), const),
                  pl.BlockSpec((1, HEAD_PAD), const),
                  pl.BlockSpec((tl, HEAD_PAD), table_map),
                  pl.BlockSpec((tl, HEAD_PAD), table_map)],
        out_specs=pl.BlockSpec((tl, hw), lambda i: (i, 0)),
        out_shape=jax.ShapeDtypeStruct((nt, hw), MXU_DTYPE),
        compiler_params=_cp(("parallel",)),
        name="q_proj",
    )(proj, norm_g, wq, wq2, qg, qg2, cos, sin)


def _rope_tables(tl, dec_seq):
    a = ROPE_DIM // 2
    t = np.arange(dec_seq)
    pos = np.stack([t // GRID_W, t % GRID_W]).astype(np.float32)
    inv = np.power(np.float32(ROPE_BASE), -np.arange(0, a, 2, dtype=np.float32) / np.float32(a))
    ang = pos[:, :, None] * inv.astype(np.float32)
    cos = np.ones((tl + dec_seq, HEAD_PAD), np.float32)
    sin = np.zeros((tl + dec_seq, HEAD_PAD), np.float32)
    hq = a // 2
    for i in range(2):
        lo = NOPE_DIM + i * a
        cos[tl:, lo:lo + hq] = np.cos(ang[i])
        cos[tl:, lo + hq:lo + a] = np.cos(ang[i])
        sin[tl:, lo:lo + hq] = -np.sin(ang[i])
        sin[tl:, lo + hq:lo + a] = np.sin(ang[i])
    return jnp.asarray(cos), jnp.asarray(sin)


def _attn_kernel(*refs, has_ctx, chunk):
    if has_ctx:
        q_ref, k_ref, v_ref, kc_ref, vc_ref, o_ref = refs
    else:
        q_ref, k_ref, v_ref, o_ref = refs
    nt_dims = (((1,), (1,)), ((), ()))
    tq = q_ref.shape[0]
    length = k_ref.shape[0]
    chunks = [(k_ref, v_ref, j, min(chunk, length - j)) for j in range(0, length, chunk)]
    if has_ctx:
        chunks = [(kc_ref, vc_ref, 0, kc_ref.shape[0])] + chunks
    heads = [slice(h * HEAD_PAD, (h + 1) * HEAD_PAD) for h in range(2)]
    qs = [q_ref[:, hs] for hs in heads]
    m = [jnp.full((tq, 1), -jnp.inf, F32) for _ in heads]
    acc = [jnp.zeros((tq, HEAD_PAD), F32) for _ in heads]
    for kr, vr, start, size in chunks:
        rows = pl.ds(start, size)
        for h, hs in enumerate(heads):
            s = lax.dot_general(qs[h], kr[rows, hs], nt_dims, preferred_element_type=F32)
            m_new = jnp.maximum(m[h], jnp.max(s, axis=-1, keepdims=True))
            p = jnp.exp2(s - m_new).astype(MXU_DTYPE)
            acc[h] = acc[h] * jnp.exp2(m[h] - m_new) + jnp.dot(p, vr[rows, hs],
                                                               preferred_element_type=F32)
            m[h] = m_new
    outs = [a[:, 0:V_DIM] / a[:, V_DIM:V_DIM + 1] for a in acc]
    o_ref[...] = jnp.concatenate(outs, axis=-1).astype(o_ref.dtype)


def _attention(q, k, v, out_rows, row_off, bsz, length, tq, ctx=None):
    assert row_off % length == 0 and length % tq == 0
    qb0 = row_off // tq
    kb0 = row_off // length
    nq = length // tq
    hp = N_HEADS // 2
    in_specs = [pl.BlockSpec((tq, 2 * HEAD_PAD), lambda b, h, i: (qb0 + b * nq + i, h)),
                pl.BlockSpec((length, 2 * HEAD_PAD), lambda b, h, i: (kb0 + b, h)),
                pl.BlockSpec((length, 2 * HEAD_PAD), lambda b, h, i: (kb0 + b, h))]
    args = [q, k, v]
    if ctx is not None:
        kc, vc, past = ctx
        in_specs += [pl.BlockSpec((past, 2 * HEAD_PAD), lambda b, h, i: (b, h)),
                     pl.BlockSpec((past, 2 * HEAD_PAD), lambda b, h, i: (b, h))]
        args += [kc, vc]
    return pl.pallas_call(
        functools.partial(_attn_kernel, has_ctx=ctx is not None, chunk=min(length, 2048)),
        grid=(bsz, hp, nq),
        in_specs=in_specs,
        out_specs=pl.BlockSpec((tq, 2 * V_DIM), lambda b, h, i: (b * nq + i, h)),
        out_shape=jax.ShapeDtypeStruct((out_rows, N_HEADS * V_DIM), MXU_DTYPE),
        compiler_params=_cp(("parallel", "parallel", "arbitrary")),
        name="attention_ctx" if ctx is None else "attention_latent",
    )(*args)


def _merge_kernel(p_ref, h_ref, a_ref, g0_ref, g1_ref, g2_ref, x_ref, mod_ref,
                  wp_ref, wh_ref, wa_ref, wo_ref, o_ref):
    yp = jnp.dot(p_ref[...], wp_ref[...], preferred_element_type=F32)
    yh = jnp.dot(h_ref[...], wh_ref[...], preferred_element_type=F32)
    ya = jnp.dot(a_ref[...], wa_ref[...], preferred_element_type=F32)
    gate = lambda ref: _sigmoid(ref[...].astype(F32))
    merged = gate(g0_ref) * yp + gate(g1_ref) * yh + gate(g2_ref) * ya
    mix = jnp.dot(merged.astype(MXU_DTYPE), wo_ref[...], preferred_element_type=F32)
    o_ref[...] = x_ref[...] + mod_ref[0][2:3] * mix


def _merge(p, hy, att, proj, x, mod, wp, wh, wa, wo, tm, ctx_rows, dec_seq):
    nt, d = x.shape
    row = lambda i: (i, 0)
    const = lambda i: (0, 0)
    g0 = OFF_G // d
    return pl.pallas_call(
        _merge_kernel,
        grid=(nt // tm,),
        in_specs=[pl.BlockSpec((tm, POOL_WIDTH), row),
                  pl.BlockSpec((tm, HY_WIDTH), row),
                  pl.BlockSpec((tm, N_HEADS * V_DIM), row),
                  pl.BlockSpec((tm, d), lambda i: (i, g0)),
                  pl.BlockSpec((tm, d), lambda i: (i, g0 + 1)),
                  pl.BlockSpec((tm, d), lambda i: (i, g0 + 2)),
                  pl.BlockSpec((tm, d), row),
                  pl.BlockSpec((1, 6, d), _mod_row_map(tm, ctx_rows, dec_seq)),
                  pl.BlockSpec((POOL_WIDTH, d), const),
                  pl.BlockSpec((HY_WIDTH, d), const),
                  pl.BlockSpec((N_HEADS * V_DIM, d), const),
                  pl.BlockSpec((d, d), const)],
        out_specs=pl.BlockSpec((tm, d), row),
        out_shape=jax.ShapeDtypeStruct((nt, d), F32),
        compiler_params=_cp(("parallel",)),
        name="merge",
    )(p, hy, att, proj, proj, proj, x, mod, wp, wh, wa, wo)


def _norm2(x_ref, mod_ref, g_ref):
    m = mod_ref[0]
    return (_rms(x_ref[...], D_MODEL) * g_ref[...]) * (1.0 + m[4:5]) + m[3:4]


def _ffn_kernel(x_ref, mod_ref, g_ref, w1_ref, w3_ref, w2_ref, o_ref, h_ref, acc_ref):
    j = pl.program_id(1)

    @pl.when(j == 0)
    def _():
        h_ref[...] = _norm2(x_ref, mod_ref, g_ref).astype(h_ref.dtype)
        acc_ref[...] = jnp.zeros_like(acc_ref)

    h = h_ref[...]
    a = jnp.dot(h, w1_ref[...], preferred_element_type=F32)
    b = jnp.dot(h, w3_ref[...], preferred_element_type=F32)
    acc_ref[...] += jnp.dot((_silu(a) * b).astype(MXU_DTYPE), w2_ref[...], preferred_element_type=F32)

    @pl.when(j == pl.num_programs(1) - 1)
    def _():
        o_ref[...] = x_ref[...] + mod_ref[0][5:6] * acc_ref[...]


def _ffn_dense(x, mod, norm_g, w1, w3, w2, tm, tf, ctx_rows, dec_seq):
    nt, d = x.shape
    ff = w1.shape[1]
    return pl.pallas_call(
        _ffn_kernel,
        grid=(nt // tm, ff // tf),
        in_specs=[pl.BlockSpec((tm, d), lambda i, j: (i, 0)),
                  pl.BlockSpec((1, 6, d), _mod_row_map(tm, ctx_rows, dec_seq)),
                  pl.BlockSpec((1, d), lambda i, j: (0, 0)),
                  pl.BlockSpec((d, tf), lambda i, j: (0, j)),
                  pl.BlockSpec((d, tf), lambda i, j: (0, j)),
                  pl.BlockSpec((tf, d), lambda i, j: (j, 0))],
        out_specs=pl.BlockSpec((tm, d), lambda i, j: (i, 0)),
        out_shape=jax.ShapeDtypeStruct((nt, d), F32),
        scratch_shapes=[pltpu.VMEM((tm, d), MXU_DTYPE), pltpu.VMEM((tm, d), F32)],
        compiler_params=_cp(("parallel", "arbitrary")),
        name="ffn_dense",
    )(x, mod, norm_g, w1, w3, w2)


def _router_kernel(x_ref, mod_ref, g_ref, r_ref, h_ref, sel_ref):
    h = _norm2(x_ref, mod_ref, g_ref)
    h_ref[...] = h
    logits = _mm_f32(h, r_ref[...])
    lane = lax.broadcasted_iota(jnp.int32, logits.shape, 1).astype(F32)
    neg = jnp.float32(-jnp.inf)
    logits = jnp.where(lane < N_EXPERTS, logits, neg)
    m1 = jnp.max(logits, axis=-1, keepdims=True)
    i1 = jnp.min(jnp.where(logits == m1, lane, float(LANE)), axis=-1, keepdims=True)
    rest = jnp.where(lane == i1, neg, logits)
    m2 = jnp.max(rest, axis=-1, keepdims=True)
    i2 = jnp.min(jnp.where(rest == m2, lane, float(LANE)), axis=-1, keepdims=True)
    e2 = jnp.exp(m2 - m1)
    w_top = 1.0 / (1.0 + e2)
    sel_ref[...] = (jnp.where(lane == 0.0, i1, 0.0) + jnp.where(lane == 1.0, i2, 0.0)
                    + jnp.where(lane == 2.0, w_top, 0.0) + jnp.where(lane == 3.0, e2 * w_top, 0.0))


def _router(x, mod, norm_g, router, tm, ctx_rows, dec_seq):
    nt, d = x.shape
    rp = jnp.pad(router, ((0, 0), (0, LANE - N_EXPERTS)))
    return pl.pallas_call(
        _router_kernel,
        grid=(nt // tm,),
        in_specs=[pl.BlockSpec((tm, d), lambda i: (i, 0)),
                  pl.BlockSpec((1, 6, d), _mod_row_map(tm, ctx_rows, dec_seq)),
                  pl.BlockSpec((1, d), lambda i: (0, 0)),
                  pl.BlockSpec((d, LANE), lambda i: (0, 0))],
        out_specs=[pl.BlockSpec((tm, d), lambda i: (i, 0)),
                   pl.BlockSpec((tm, LANE), lambda i: (i, 0))],
        out_shape=[jax.ShapeDtypeStruct((nt, d), F32),
                   jax.ShapeDtypeStruct((nt, LANE), F32)],
        compiler_params=_cp(("parallel",)),
        name="router",
    )(x, mod, norm_g, rp)


def _dispatch_plan(sel, rows_per_tile):
    nt = sel.shape[0]
    n_pairs = 2 * nt
    r = rows_per_tile
    n_slots = n_pairs + N_EXPERTS * r
    n_tiles = n_slots // r
    e_p = sel[:, 0:2].astype(jnp.int32).T.reshape(n_pairs)
    w_p = sel[:, 2:4].T.reshape(n_pairs)
    onehot = (e_p[:, None] == jnp.arange(N_EXPERTS, dtype=jnp.int32)[None, :]).astype(jnp.int32)
    csum = jnp.cumsum(onehot, axis=0)
    rank = jnp.sum((csum - onehot) * onehot, axis=1)
    counts = csum[-1]
    padded = (counts + r - 1) // r * r
    ends = jnp.cumsum(padded)
    starts = ends - padded
    slot_p = jnp.sum(onehot * starts[None, :], axis=1) + rank
    pair_ids = jnp.arange(n_pairs, dtype=jnp.int32)
    slot_pair = jnp.full((n_slots,), -1, jnp.int32).at[slot_p].set(pair_ids)
    slot_w = jnp.zeros((n_slots,), F32).at[slot_p].set(w_p)
    empty = slot_pair < 0
    spare = n_pairs + jnp.cumsum(empty.astype(jnp.int32)) - 1
    slot_row = jnp.where(empty, spare, slot_pair)
    slot_tok = jnp.where(empty, 0, slot_pair % nt)
    tile_start = jnp.arange(n_tiles, dtype=jnp.int32) * r
    tile_expert = jnp.minimum(jnp.sum((tile_start[:, None] >= ends[None, :]).astype(jnp.int32), axis=1),
                              N_EXPERTS - 1)
    return (tile_expert, slot_tok.reshape(n_tiles, 1, r), slot_row.reshape(n_tiles, 1, r),
            slot_w.reshape(n_slots, 1))


def _row_copies(src, src_rows, dst, dst_rows, sem, n):
    return [pltpu.make_async_copy(src.at[pl.ds(src_rows(i), 1), :], dst.at[pl.ds(dst_rows(i), 1), :], sem)
            for i in range(n)]


def _moe_kernel(te_ref, tok_ref, tok_next_ref, row_ref, row_prev_ref, h_hbm, sw_ref, w1_ref, w3_ref,
                w2_ref, y_hbm, gbuf, obuf, gsem, osem, *, rows, sub):
    t = pl.program_id(0)
    last = pl.num_programs(0) - 1
    slot = t % 2
    other = 1 - slot

    def gather(idx_ref, s):
        return _row_copies(h_hbm, lambda i: idx_ref[0, 0, i], gbuf.at[s], lambda i: i, gsem.at[s], rows)

    def scatter(idx_ref, s):
        return _row_copies(obuf.at[s], lambda i: i, y_hbm, lambda i: idx_ref[0, 0, i], osem.at[s], rows)

    @pl.when(t == 0)
    def _():
        for cp in gather(tok_ref, slot):
            cp.start()

    for cp in gather(tok_next_ref, other):
        cp.start()
    for cp in gather(tok_ref, slot):
        cp.wait()

    for j in range(rows // sub):
        rs = pl.ds(j * sub, sub)
        xb = gbuf[slot, rs, :].astype(MXU_DTYPE)
        a = jnp.dot(xb, w1_ref[0], preferred_element_type=F32)
        b = jnp.dot(xb, w3_ref[0], preferred_element_type=F32)
        y = jnp.dot((_silu(a) * b).astype(MXU_DTYPE), w2_ref[0], preferred_element_type=F32)
        obuf[slot, rs, :] = y * sw_ref[rs, :]

    @pl.when(t > 0)
    def _():
        for cp in scatter(row_prev_ref, other):
            cp.wait()

    for cp in scatter(row_ref, slot):
        cp.start()

    @pl.when(t == last)
    def _():
        for cp in scatter(row_ref, slot):
            cp.wait()
        for cp in gather(tok_next_ref, other):
            cp.wait()


def _ffn_moe(h, plan, w1, w3, w2, rows):
    tile_expert, slot_tok, slot_row, slot_w = plan
    nt, d = h.shape
    n_tiles = slot_tok.shape[0]
    _, _, ff = w1.shape
    idx_spec = lambda f: pl.BlockSpec((1, 1, rows), f, memory_space=pltpu.SMEM)
    cur = lambda t, te: (t, 0, 0)
    nxt = lambda t, te: (jnp.minimum(t + 1, n_tiles - 1), 0, 0)
    prv = lambda t, te: (jnp.maximum(t - 1, 0), 0, 0)
    wspec = lambda shape: pl.BlockSpec((1,) + shape, lambda t, te: (te[t], 0, 0))
    grid_spec = pltpu.PrefetchScalarGridSpec(
        num_scalar_prefetch=1,
        grid=(n_tiles,),
        in_specs=[idx_spec(cur), idx_spec(nxt), idx_spec(cur), idx_spec(prv),
                  pl.BlockSpec(memory_space=pl.ANY),
                  pl.BlockSpec((rows, 1), lambda t, te: (t, 0)),
                  wspec((d, ff)), wspec((d, ff)), wspec((ff, d))],
        out_specs=pl.BlockSpec(memory_space=pl.ANY),
        scratch_shapes=[pltpu.VMEM((2, rows, d), F32), pltpu.VMEM((2, rows, d), F32),
                        pltpu.SemaphoreType.DMA((2,)), pltpu.SemaphoreType.DMA((2,))])
    return pl.pallas_call(
        functools.partial(_moe_kernel, rows=rows, sub=min(rows, 256)),
        grid_spec=grid_spec,
        out_shape=jax.ShapeDtypeStruct((n_tiles * rows, d), F32),
        compiler_params=_cp(("arbitrary",)),
        name="ffn_moe",
    )(tile_expert, slot_tok, slot_tok, slot_row, slot_row, h, slot_w, w1, w3, w2)


def _combine_kernel(x_ref, mod_ref, y0_ref, y1_ref, o_ref):
    o_ref[...] = x_ref[...] + mod_ref[0][5:6] * (y0_ref[...] + y1_ref[...])


def _moe_combine(x, mod, y, tm, ctx_rows, dec_seq):
    nt, d = x.shape
    nb = nt // tm
    return pl.pallas_call(
        _combine_kernel,
        grid=(nb,),
        in_specs=[pl.BlockSpec((tm, d), lambda i: (i, 0)),
                  pl.BlockSpec((1, 6, d), _mod_row_map(tm, ctx_rows, dec_seq)),
                  pl.BlockSpec((tm, d), lambda i: (i, 0)),
                  pl.BlockSpec((tm, d), lambda i: (nb + i, 0))],
        out_specs=pl.BlockSpec((tm, d), lambda i: (i, 0)),
        out_shape=jax.ShapeDtypeStruct((nt, d), F32),
        compiler_params=_cp(("parallel",)),
        name="moe_combine",
    )(x, mod, y, y)


def _prep_w_in(w_in):
    depth, d, _ = w_in.shape
    c1 = POOL_WIDTH
    c2 = c1 + 3 * HY_WIDTH
    c3 = c2 + Q_RANK
    c4 = c3 + KV_RANK
    c5 = c4 + ROPE_DIM
    z = lambda n: jnp.zeros((depth, d, n), w_in.dtype)
    kr_group = jnp.concatenate([z(NOPE_DIM), w_in[:, :, c4:c5], z(HEAD_PAD - QK_DIM)], axis=-1)
    cols = [w_in[:, :, c1:c2], w_in[:, :, 0:c1], w_in[:, :, c3:c4],
            kr_group, _partner_lanes(kr_group),
            w_in[:, :, c2:c3], z(Q_PAD - Q_RANK), w_in[:, :, c5:]]
    out = jnp.concatenate(cols, axis=-1).astype(MXU_DTYPE)
    assert out.shape[-1] == IN_PAD
    return out


def _head_pad_cols(w, width):
    lead = w.shape[:-1]
    w = w.reshape(*lead, N_HEADS, width)
    w = jnp.pad(w, [(0, 0)] * len(lead) + [(0, 0), (0, HEAD_PAD - width)])
    return w.reshape(*lead, N_HEADS * HEAD_PAD)


def _pad_last(a, n):
    return jnp.pad(a, [(0, 0)] * (a.ndim - 1) + [(0, n - a.shape[-1])])


def kernel(x_prompt, x_sample, cache_ckv, cache_krope, c, c_ctx, w_mod, b_mod, norm1_g, norm2_g, w_in, pool_w, pool_scale, pool_out, hy_conv_w, hy_conv_b, hy_w1, hy_b1, hy_w2, hy_b2, hy_w3, hy_freq, hy_decay, hy_bias, hy_out, q_norm_g, w_qb, kv_norm_g, w_kvb, qk_q_g, qk_k_g, mla_out, w_out, ffn_w1, ffn_w3, ffn_w2, moe_router, moe_w1, moe_w3, moe_w2):
    batch, seq, d = x_prompt.shape
    dec_batch, dec_seq, _ = x_sample.shape
    depth = w_mod.shape[0]
    past = cache_ckv.shape[2]
    ctx_rows = batch * seq
    lat_rows = dec_batch * dec_seq
    nt = ctx_rows + lat_rows
    tl = 256
    assert seq % tl == 0 and dec_seq % tl == 0 and ctx_rows % dec_seq == 0 and past % tl == 0
    tm = min(1024, math.gcd(ctx_rows, dec_seq))
    n_ctx_tiles = ctx_rows // tl
    bf = lambda a: a.astype(MXU_DTYPE)

    w_in_p = _prep_w_in(w_in)
    eye = jnp.eye(POOL_GROUPS, dtype=pool_w.dtype)
    gc = POOL_WIDTH // POOL_GROUPS
    pool_bd = bf(jnp.einsum('lgcd,gh->lgchd', pool_w, eye).reshape(depth, POOL_WIDTH, POOL_WIDTH))
    kvb = w_kvb.reshape(depth, KV_RANK, N_HEADS, NOPE_DIM + V_DIM)
    wk = bf(_head_pad_cols(kvb[..., :NOPE_DIM].reshape(depth, KV_RANK, N_HEADS * NOPE_DIM), NOPE_DIM))
    wv = bf(_head_pad_cols(kvb[..., NOPE_DIM:].reshape(depth, KV_RANK, N_HEADS * V_DIM), V_DIM))
    vone = jnp.tile((jnp.arange(HEAD_PAD) == V_DIM).astype(F32), N_HEADS)[None, :]
    wq_f = jnp.pad(_head_pad_cols(w_qb, QK_DIM), ((0, 0), (0, Q_PAD - Q_RANK), (0, 0)))
    wq, wq2 = bf(wq_f), bf(_partner_lanes(wq_f))
    q_norm_p = _pad_last(q_norm_g, Q_PAD)
    qg = _pad_last(qk_q_g, HEAD_PAD)
    kg = _pad_last(qk_k_g, HEAD_PAD)
    qg2, kg2 = _partner_lanes(qg), _partner_lanes(kg)
    pool_out_b, hy_out_b, mla_out_b, w_out_b = bf(pool_out), bf(hy_out), bf(mla_out), bf(w_out)
    ffn_w1_b, ffn_w3_b, ffn_w2_b = bf(ffn_w1), bf(ffn_w3), bf(ffn_w2)
    moe_w1_b, moe_w3_b, moe_w2_b = bf(moe_w1), bf(moe_w3), bf(moe_w2)

    cos_t, sin_t = _rope_tables(tl, dec_seq)
    n_lat_tiles = dec_seq // tl
    table_map = lambda i: (jnp.where(i < n_ctx_tiles, 0, 1 + (i - n_ctx_tiles) % n_lat_tiles), 0)
    ident_map = lambda i: (0, 0)
    mats_ctx = _dft_matrices(seq)
    mats_lat = _dft_matrices(dec_seq)

    n_cond = 1 + dec_batch
    cond_rows = -(-n_cond // SUBLANE) * SUBLANE
    cond = jnp.concatenate([c_ctx[None, :], c, jnp.zeros((cond_rows - n_cond, d), c.dtype)], axis=0)
    mod_all = _modulation(cond, w_mod, b_mod)[:, :n_cond].reshape(depth, n_cond, 6, d)
    filt_ctx = _hyena_filters(seq, hy_w1, hy_b1, hy_w2, hy_b2, hy_w3, hy_freq, hy_decay)
    filt_lat = _hyena_filters(dec_seq, hy_w1, hy_b1, hy_w2, hy_b2, hy_w3, hy_freq, hy_decay)
    cache_kr_p = jnp.pad(cache_krope, ((0, 0), (0, 0), (0, 0), (NOPE_DIM, HEAD_PAD - QK_DIM)))
    cache_kr_p = jnp.concatenate([cache_kr_p, _partner_lanes(cache_kr_p)], axis=-1)

    x = jnp.concatenate([x_prompt.reshape(ctx_rows, d), x_sample.reshape(lat_rows, d)], axis=0)
    ckv_list, kr_list = [], []
    for i in range(depth):
        j = i // 2
        mod = mod_all[i]
        r1 = lambda a: a[i].reshape(1, -1)
        proj = _inproj(x, mod, r1(norm1_g), w_in_p[i], tm, ctx_rows, dec_seq)

        vx, vxb, x0, pooled = _local_mixers(proj, hy_conv_w[i], r1(hy_conv_b), pool_bd[i], r1(pool_scale),
                                            tl, n_ctx_tiles, seq, dec_seq)
        bias = r1(hy_bias)
        hy_c = _long_conv(vx[:ctx_rows], vxb[:ctx_rows], x0[:ctx_rows], filt_ctx[i], bias,
                          mats_ctx, batch, seq)
        hy_l = _long_conv(vx[ctx_rows:], vxb[ctx_rows:], x0[ctx_rows:], filt_lat[i], bias,
                          mats_lat, dec_batch, dec_seq)
        hy = jnp.concatenate([hy_c, hy_l], axis=0)

        ckv, k, v = _kv_proj(proj, OFF_KV // KV_RANK, proj, OFF_KROPE // HEAD_PAD, OFF_KROPE2 // HEAD_PAD,
                             r1(kv_norm_g), wk[i], wv[i], vone, r1(kg), r1(kg2), cos_t, sin_t,
                             table_map, tl, True)
        kc, vc = _kv_proj(cache_ckv[:, i].reshape(dec_batch * past, KV_RANK), 0,
                          cache_kr_p[:, i].reshape(dec_batch * past, 2 * HEAD_PAD), 0, 1,
                          r1(kv_norm_g), wk[i], wv[i], vone, r1(kg), r1(kg2), cos_t, sin_t,
                          ident_map, tl, False)
        q = _q_proj(proj, r1(q_norm_p), wq[i], wq2[i], r1(qg), r1(qg2), cos_t, sin_t, table_map, tl)
        att_c = _attention(q, k, v, ctx_rows, 0, batch, seq, min(seq, 256))
        att_l = _attention(q, k, v, lat_rows, ctx_rows, dec_batch, dec_seq, min(dec_seq, 512),
                           ctx=(kc, vc, past))
        att = jnp.concatenate([att_c, att_l], axis=0)

        x = _merge(pooled, hy, att, proj, x, mod, pool_out_b[i], hy_out_b[i], mla_out_b[i], w_out_b[i],
                   min(tm, 512), ctx_rows, dec_seq)

        if i % 2 == 1:
            h2, sel = _router(x, mod, r1(norm2_g), moe_router[j], tm, ctx_rows, dec_seq)
            y = _ffn_moe(h2, _dispatch_plan(sel, MOE_ROWS), moe_w1_b[j], moe_w3_b[j], moe_w2_b[j],
                         MOE_ROWS)
            x = _moe_combine(x, mod, y, tm, ctx_rows, dec_seq)
        else:
            x = _ffn_dense(x, mod, r1(norm2_g), ffn_w1_b[j], ffn_w3_b[j], ffn_w2_b[j],
                           tm, ffn_w1.shape[-1] // 2, ctx_rows, dec_seq)

        ckv_list.append(ckv[:ctx_rows].reshape(batch, seq, KV_RANK))
        kr_list.append(proj[:ctx_rows, OFF_KROPE + NOPE_DIM:OFF_KROPE + QK_DIM].astype(F32)
                       .reshape(batch, seq, ROPE_DIM))

    y_p = x[:ctx_rows].reshape(batch, seq, d)
    y_s = x[ctx_rows:].reshape(dec_batch, dec_seq, d)
    return (y_p, y_s, jnp.stack(ckv_list, axis=1), jnp.stack(kr_list, axis=1))
```

```python
import functools
import math

import numpy as np
import jax
import jax.numpy as jnp
from jax import lax
from jax.experimental import pallas as pl
from jax.experimental.pallas import tpu as pltpu

F32 = jnp.float32
MXU_DTYPE = jnp.bfloat16
HIGHEST = lax.Precision.HIGHEST

D_MODEL = 1024
GRID_W = 64
EPS = 1e-6
POOL_WIDTH = 256
POOL_GROUPS = 4
POOL_WINDOWS = (2, 4, 8, 16)
HY_WIDTH = 256
HY_SHORT = 3
HY_BANDS = 8
HY_EMB = 1 + 2 * HY_BANDS
HY_FFN = 64
N_HEADS = 8
Q_RANK = 384
KV_RANK = 256
NOPE_DIM = 64
ROPE_DIM = 32
V_DIM = 64
QK_DIM = NOPE_DIM + ROPE_DIM
ROPE_BASE = 10000.0
N_EXPERTS = 8

LANE = 128
SUBLANE = 8
HALO = 16
HEAD_PAD = LANE
Q_PAD = 512
MOE_ROWS = 512
VMEM_LIMIT = 56 * 1024 * 1024
SOFTMAX_C = (QK_DIM ** -0.5) * math.log2(math.e)

OFF_HY = 0
OFF_POOL = 3 * HY_WIDTH
OFF_KV = OFF_POOL + POOL_WIDTH
OFF_KROPE = OFF_KV + KV_RANK
OFF_KROPE2 = OFF_KROPE + HEAD_PAD
OFF_Q = 1536
OFF_G = OFF_Q + Q_PAD
IN_PAD = OFF_G + 3 * D_MODEL


def _cp(sem, vmem=VMEM_LIMIT):
    return pltpu.CompilerParams(dimension_semantics=sem, vmem_limit_bytes=vmem)


def _mm(a, b):
    return jnp.dot(a.astype(MXU_DTYPE), b.astype(MXU_DTYPE), preferred_element_type=F32)


def _mm_f32(a, b):
    return jnp.dot(a, b, preferred_element_type=F32, precision=HIGHEST)


def _sigmoid(x):
    return 1.0 / (1.0 + jnp.exp(-x))


def _silu(x):
    return x * _sigmoid(x)


def _rms(x, n):
    ms = jnp.sum(x * x, axis=-1, keepdims=True) * (1.0 / n)
    return x * lax.rsqrt(ms + EPS)


def _mod_row_map(tm, ctx_rows, dec_seq):
    def index_map(i, *_):
        row0 = i * tm
        return (jnp.where(row0 < ctx_rows, 0, 1 + (row0 - ctx_rows) // dec_seq), 0, 0)
    return index_map


def _mod_kernel(c_ref, w_ref, b_ref, o_ref):
    o_ref[0] = _mm_f32(_silu(c_ref[...]), w_ref[0]) + b_ref[0]


def _modulation(cond, w_mod, b_mod):
    depth, d, n6 = w_mod.shape
    r = cond.shape[0]
    tn = 1024
    return pl.pallas_call(
        _mod_kernel,
        grid=(depth, n6 // tn),
        in_specs=[pl.BlockSpec((r, d), lambda l, j: (0, 0)),
                  pl.BlockSpec((1, d, tn), lambda l, j: (l, 0, j)),
                  pl.BlockSpec((1, 1, tn), lambda l, j: (l, 0, j))],
        out_specs=pl.BlockSpec((1, r, tn), lambda l, j: (l, 0, j)),
        out_shape=jax.ShapeDtypeStruct((depth, r, n6), F32),
        compiler_params=_cp(("parallel", "parallel")),
        name="modulation",
    )(cond, w_mod, b_mod.reshape(depth, 1, n6))


def _inproj_kernel(x_ref, mod_ref, g_ref, w_ref, o_ref, h_ref):
    @pl.when(pl.program_id(1) == 0)
    def _():
        m = mod_ref[0]
        h = _rms(x_ref[...], D_MODEL) * g_ref[...]
        h_ref[...] = (h * (1.0 + m[1:2]) + m[0:1]).astype(h_ref.dtype)
    o_ref[...] = jnp.dot(h_ref[...], w_ref[...], preferred_element_type=F32).astype(o_ref.dtype)


def _inproj(x, mod, norm_g, w, tm, ctx_rows, dec_seq):
    nt, d = x.shape
    n = w.shape[1]
    tn = 1024
    return pl.pallas_call(
        _inproj_kernel,
        grid=(nt // tm, n // tn),
        in_specs=[pl.BlockSpec((tm, d), lambda i, j: (i, 0)),
                  pl.BlockSpec((1, 6, d), _mod_row_map(tm, ctx_rows, dec_seq)),
                  pl.BlockSpec((1, d), lambda i, j: (0, 0)),
                  pl.BlockSpec((d, tn), lambda i, j: (0, j))],
        out_specs=pl.BlockSpec((tm, tn), lambda i, j: (i, j)),
        out_shape=jax.ShapeDtypeStruct((nt, n), MXU_DTYPE),
        scratch_shapes=[pltpu.VMEM((tm, d), MXU_DTYPE)],
        compiler_params=_cp(("parallel", "arbitrary")),
        name="inproj",
    )(x, mod, norm_g, w)


def _local_kernel(u_ref, prev_ref, next_ref, cw_ref, cb_ref, pw_ref, ps_ref,
                  vx_ref, vxb_ref, x0_ref, p_ref, ext_ref, *, tl, n_ctx_tiles, seq, dec_seq):
    i = pl.program_id(0)
    row0 = i * tl
    ctx = i < n_ctx_tiles
    p0 = jnp.where(ctx, row0 % seq, (row0 - n_ctx_tiles * tl) % dec_seq)
    length = jnp.where(ctx, seq, dec_seq)
    first = p0 == 0
    last = p0 + tl == length

    ext_ref[0:SUBLANE, :] = jnp.where(first, 0.0, prev_ref[HALO - SUBLANE:HALO, :].astype(F32))
    ext_ref[SUBLANE:SUBLANE + tl, :] = u_ref[...].astype(F32)
    ext_ref[SUBLANE + tl:, :] = jnp.where(last, 0.0, next_ref[0:SUBLANE, :].astype(F32))

    hw = 3 * HY_WIDTH
    cw = cw_ref[...]
    z = (cb_ref[...]
         + ext_ref[pl.ds(SUBLANE - 1, tl), 0:hw] * cw[0:1]
         + ext_ref[pl.ds(SUBLANE, tl), 0:hw] * cw[1:2]
         + ext_ref[pl.ds(SUBLANE + 1, tl), 0:hw] * cw[2:3])
    x0 = z[:, 0:HY_WIDTH]
    vx = z[:, 2 * HY_WIDTH:3 * HY_WIDTH] * z[:, HY_WIDTH:2 * HY_WIDTH]
    vx_ref[...] = vx
    vxb_ref[...] = vx.astype(vxb_ref.dtype)
    x0_ref[...] = x0

    def e(j):
        return ext_ref[pl.ds(SUBLANE + j, tl), hw:hw + POOL_WIDTH]

    u = e(0)
    s2 = e(-1) + u
    s4 = s2 + e(-2) + e(1)
    s8 = s4 + e(-4) + e(-3) + e(2) + e(3)
    s16 = s8 + e(-8) + e(-7) + e(-6) + e(-5) + e(4) + e(5) + e(6) + e(7)
    lane = lax.broadcasted_iota(jnp.int32, (tl, POOL_WIDTH), 1)
    gc = POOL_WIDTH // POOL_GROUPS
    ssum = jnp.where(lane < gc, s2, jnp.where(lane < 2 * gc, s4, jnp.where(lane < 3 * gc, s8, s16)))
    half = jnp.where(lane < gc, 1, jnp.where(lane < 2 * gc, 2, jnp.where(lane < 3 * gc, 4, 8)))
    t = p0 + lax.broadcasted_iota(jnp.int32, (tl, POOL_WIDTH), 0)
    cnt = jnp.minimum(t + half, length) - jnp.maximum(t - half, 0)
    pooled = ssum / cnt.astype(F32) - u
    p_ref[...] = (_mm(pooled, pw_ref[...]) * ps_ref[...]).astype(p_ref.dtype)


def _local_mixers(proj, conv_w, conv_b, pool_bd, pool_scale, tl, n_ctx_tiles, seq, dec_seq):
    nt = proj.shape[0]
    width = 3 * HY_WIDTH + POOL_WIDTH
    nb = tl // HALO
    last_blk = nt // HALO - 1
    kern = functools.partial(_local_kernel, tl=tl, n_ctx_tiles=n_ctx_tiles, seq=seq, dec_seq=dec_seq)
    row = lambda i: (i, 0)
    const = lambda i: (0, 0)
    return pl.pallas_call(
        kern,
        grid=(nt // tl,),
        in_specs=[pl.BlockSpec((tl, width), row),
                  pl.BlockSpec((HALO, width), lambda i: (jnp.maximum(i * nb - 1, 0), 0)),
                  pl.BlockSpec((HALO, width), lambda i: (jnp.minimum((i + 1) * nb, last_blk), 0)),
                  pl.BlockSpec((HY_SHORT, 3 * HY_WIDTH), const),
                  pl.BlockSpec((1, 3 * HY_WIDTH), const),
                  pl.BlockSpec((POOL_WIDTH, POOL_WIDTH), const),
                  pl.BlockSpec((1, POOL_WIDTH), const)],
        out_specs=[pl.BlockSpec((tl, HY_WIDTH), row)] * 3 + [pl.BlockSpec((tl, POOL_WIDTH), row)],
        out_shape=[jax.ShapeDtypeStruct((nt, HY_WIDTH), F32),
                   jax.ShapeDtypeStruct((nt, HY_WIDTH), MXU_DTYPE),
                   jax.ShapeDtypeStruct((nt, HY_WIDTH), F32),
                   jax.ShapeDtypeStruct((nt, POOL_WIDTH), MXU_DTYPE)],
        scratch_shapes=[pltpu.VMEM((tl + 2 * SUBLANE, width), F32)],
        compiler_params=_cp(("parallel",)),
        name="local_mixers",
    )(proj, proj, proj, conv_w, conv_b, pool_bd, pool_scale)


def _filter_kernel(z_ref, w1_ref, b1_ref, w2_ref, b2_ref, w3_ref, fr_ref, dec_ref, o_ref):
    z = z_ref[...]
    fr = fr_ref[0]
    hdn = jnp.sin(fr * (_mm_f32(z, w1_ref[0]) + b1_ref[0]))
    hdn = jnp.sin(fr * (_mm_f32(hdn, w2_ref[0]) + b2_ref[0]))
    filt = _mm_f32(hdn, w3_ref[0])
    win = jnp.exp(-z[:, 0:1] * jnp.abs(dec_ref[0]))
    o_ref[0] = filt * jnp.concatenate([win, win], axis=-1)


def _hyena_filters(length, w1, b1, w2, b2, w3, freq, decay):
    depth = w1.shape[0]
    t = np.linspace(0.0, 1.0, length, dtype=np.float32)[:, None]
    wpos = (np.float32(2.0 * math.pi / length) * np.arange(length, dtype=np.float32))[:, None]
    bands = np.linspace(1e-4, HY_BANDS - 1, HY_BANDS, dtype=np.float32)[None, :]
    emb_pad = 32
    z = np.zeros((length, emb_pad), np.float32)
    z[:, 0:1] = t
    z[:, 1:1 + HY_BANDS] = np.cos(bands * wpos)
    z[:, 1 + HY_BANDS:HY_EMB] = -np.sin(bands * wpos)
    w1p = jnp.pad(w1, ((0, 0), (0, emb_pad - HY_EMB), (0, 0)))
    tl = min(length, 512)
    lay = lambda l, i: (l, 0, 0)
    r3 = lambda a: a.reshape(depth, 1, a.shape[-1])
    return pl.pallas_call(
        _filter_kernel,
        grid=(depth, length // tl),
        in_specs=[pl.BlockSpec((tl, emb_pad), lambda l, i: (i, 0)),
                  pl.BlockSpec((1, emb_pad, HY_FFN), lay),
                  pl.BlockSpec((1, 1, HY_FFN), lay),
                  pl.BlockSpec((1, HY_FFN, HY_FFN), lay),
                  pl.BlockSpec((1, 1, HY_FFN), lay),
                  pl.BlockSpec((1, HY_FFN, 2 * HY_WIDTH), lay),
                  pl.BlockSpec((1, 1, HY_FFN), lay),
                  pl.BlockSpec((1, 1, HY_WIDTH), lay)],
        out_specs=pl.BlockSpec((1, tl, 2 * HY_WIDTH), lambda l, i: (l, i, 0)),
        out_shape=jax.ShapeDtypeStruct((depth, length, 2 * HY_WIDTH), F32),
        compiler_params=_cp(("parallel", "parallel")),
        name="hyena_filters",
    )(jnp.asarray(z), w1p, r3(b1), w2, r3(b2), w3, r3(freq), r3(decay))


def _dft_matrices(length):
    n = 2 * length
    f = jnp.arange(length, dtype=jnp.int32)[:, None]
    s = jnp.arange(length, dtype=jnp.int32)[None, :]
    blk = 64
    theta = np.float32(2.0 * math.pi / n)
    a_hi = ((f * (s[:, 0:length // blk] * blk)) % n).astype(F32) * theta
    a_lo = ((f * s[:, 0:blk]) % n).astype(F32) * theta
    c_hi, s_hi = jnp.cos(a_hi)[:, :, None], jnp.sin(a_hi)[:, :, None]
    c_lo, s_lo = jnp.cos(a_lo)[:, None, :], jnp.sin(a_lo)[:, None, :]
    cos = (c_hi * c_lo - s_hi * s_lo).reshape(length, length)
    sin = (s_hi * c_lo + c_hi * s_lo).reshape(length, length)
    alt = jnp.where(jnp.arange(length) % 2 == 0, 1.0, -1.0).astype(F32)
    fwd_im = jnp.where(f == 0, alt[None, :], -sin)
    fwd = jnp.concatenate([cos, fwd_im], axis=0)
    inv_re = jnp.where(s == 0, 1.0 / n, (2.0 / n) * cos)
    inv_im = jnp.where(s == 0, alt[:, None] / n, (-2.0 / n) * sin)
    inv = jnp.concatenate([inv_re, inv_im], axis=1)
    return fwd.astype(MXU_DTYPE), inv.astype(MXU_DTYPE)


def _dft_fwd_kernel(g_ref, x_ref, o_ref, *, bg):
    @pl.when(pl.program_id(2) == 0)
    def _():
        o_ref[...] = jnp.zeros_like(o_ref)
    g = g_ref[...]
    for b in range(bg):
        o_ref[b] += jnp.dot(g, x_ref[b], preferred_element_type=F32)


def _dft_fwd(gmat, x, bg):
    bsz, length, cols = x.shape
    tm = min(2 * length, 1024)
    tk = min(length, 512)
    return pl.pallas_call(
        functools.partial(_dft_fwd_kernel, bg=bg),
        grid=(bsz // bg, 2 * length // tm, length // tk),
        in_specs=[pl.BlockSpec((tm, tk), lambda g, m, k: (m, k)),
                  pl.BlockSpec((bg, tk, cols), lambda g, m, k: (g, k, 0))],
        out_specs=pl.BlockSpec((bg, tm, cols), lambda g, m, k: (g, m, 0)),
        out_shape=jax.ShapeDtypeStruct((bsz, 2 * length, cols), F32),
        compiler_params=_cp(("parallel", "parallel", "arbitrary")),
        name="dft_fwd",
    )(gmat, x)


def _dft_inv_kernel(gc_ref, gs_ref, xr_ref, xi_ref, ar_ref, ai_ref, hb0_ref, vx_ref, x0_ref,
                    bias_ref, o_ref, acc_ref, *, bg, tk):
    k = pl.program_id(2)

    @pl.when(k == 0)
    def _():
        acc_ref[...] = jnp.zeros_like(acc_ref)

    c = HY_WIDTH
    ar = ar_ref[...]
    ai = ai_ref[...]
    hb0 = hb0_ref[...]
    f0 = (k * tk + lax.broadcasted_iota(jnp.int32, (tk, c), 0)) == 0
    kr = ar[:, 0:c] + ar[:, c:2 * c] - hb0
    ki = jnp.where(f0, ai[:, 0:c] + ai[:, c:2 * c] - hb0, ai[:, 0:c] - ai[:, c:2 * c])
    gc = gc_ref[...]
    gs = gs_ref[...]
    for b in range(bg):
        xr = xr_ref[b]
        xi = xi_ref[b]
        zr = jnp.where(f0, xr * kr, xr * kr - xi * ki)
        zi = jnp.where(f0, xi * ki, xr * ki + xi * kr)
        acc_ref[b] += (jnp.dot(gc, zr.astype(MXU_DTYPE), preferred_element_type=F32)
                       + jnp.dot(gs, zi.astype(MXU_DTYPE), preferred_element_type=F32))

    @pl.when(k == pl.num_programs(2) - 1)
    def _():
        y = (acc_ref[...] + vx_ref[...] * bias_ref[...]) * x0_ref[...]
        o_ref[...] = y.astype(o_ref.dtype)


def _dft_inv(ginv, xspec, fspec, hb0, vx, x0, bias, bg):
    bsz, length, c = vx.shape
    tm = min(length, 512)
    tk = min(length, 512)
    nk = length // tk
    return pl.pallas_call(
        functools.partial(_dft_inv_kernel, bg=bg, tk=tk),
        grid=(bsz // bg, length // tm, nk),
        in_specs=[pl.BlockSpec((tm, tk), lambda g, m, k: (m, k)),
                  pl.BlockSpec((tm, tk), lambda g, m, k: (m, nk + k)),
                  pl.BlockSpec((bg, tk, c), lambda g, m, k: (g, k, 0)),
                  pl.BlockSpec((bg, tk, c), lambda g, m, k: (g, nk + k, 0)),
                  pl.BlockSpec((tk, 2 * c), lambda g, m, k: (k, 0)),
                  pl.BlockSpec((tk, 2 * c), lambda g, m, k: (nk + k, 0)),
                  pl.BlockSpec((1, c), lambda g, m, k: (0, 0)),
                  pl.BlockSpec((bg, tm, c), lambda g, m, k: (g, m, 0)),
                  pl.BlockSpec((bg, tm, c), lambda g, m, k: (g, m, 0)),
                  pl.BlockSpec((1, c), lambda g, m, k: (0, 0))],
        out_specs=pl.BlockSpec((bg, tm, c), lambda g, m, k: (g, m, 0)),
        out_shape=jax.ShapeDtypeStruct((bsz, length, c), MXU_DTYPE),
        scratch_shapes=[pltpu.VMEM((bg, tm, c), F32)],
        compiler_params=_cp(("parallel", "parallel", "arbitrary")),
        name="dft_inv",
    )(ginv, ginv, xspec, xspec, fspec, fspec, hb0, vx, x0, bias)


def _long_conv(vx, vxb, x0, filt, bias, mats, bsz, length):
    gfwd, ginv = mats
    c = vx.shape[-1]
    bg = math.gcd(bsz, 8)
    fspec = _dft_fwd(gfwd, filt.astype(MXU_DTYPE)[None], 1)[0]
    xspec = _dft_fwd(gfwd, vxb.reshape(bsz, length, c), bg)
    hb0 = filt[0:1, c:2 * c]
    out = _dft_inv(ginv, xspec, fspec, hb0, vx.reshape(bsz, length, c), x0.reshape(bsz, length, c),
                   bias, bg)
    return out.reshape(bsz * length, c)


def _rope_partner(r):
    a = ROPE_DIM // 2
    return a * (r // a) + (r % a + a // 2) % a


def _partner_lanes(a):
    idx = np.arange(a.shape[-1])
    r = idx % HEAD_PAD - NOPE_DIM
    rot = (r >= 0) & (r < ROPE_DIM)
    src = np.where(rot, idx - r + _rope_partner(np.clip(r, 0, ROPE_DIM - 1)), idx)
    return jnp.where(jnp.asarray(rot), a[..., src], jnp.zeros((), a.dtype))


def _head_norm_rope(x, xp, cg, sg, out_scale):
    ms = jnp.sum(x * x, axis=-1, keepdims=True) * (1.0 / QK_DIM)
    return (x * cg + xp * sg) * (lax.rsqrt(ms + EPS) * out_scale)


def _kv_kernel(kva_ref, kr_ref, kr2_ref, ng_ref, wk_ref, wv_ref, vone_ref, kg_ref, kg2_ref,
               cos_ref, sin_ref, ckv_ref, k_ref, v_ref, *, normalize):
    kva = kva_ref[...].astype(F32)
    ckv = _rms(kva, KV_RANK) * ng_ref[...] if normalize else kva
    if normalize:
        ckv_ref[...] = ckv
    cb = ckv.astype(MXU_DTYPE)
    kfull = jnp.dot(cb, wk_ref[...], preferred_element_type=F32)
    kr = kr_ref[...].astype(F32)
    cg = cos_ref[...] * kg_ref[...]
    sg = sin_ref[...] * kg2_ref[...]
    xp = kr2_ref[...].astype(F32)
    for h in range(N_HEADS):
        sl = slice(h * HEAD_PAD, (h + 1) * HEAD_PAD)
        k_ref[:, sl] = _head_norm_rope(kfull[:, sl] + kr, xp, cg, sg, 1.0).astype(k_ref.dtype)
    v = jnp.dot(cb, wv_ref[...], preferred_element_type=F32) + vone_ref[...]
    v_ref[...] = v.astype(v_ref.dtype)


def _kv_proj(src, kv_blk, kr_src, kr_blk, kr2_blk, norm_g, wk, wv, vone, kg, kg2, cos, sin,
             table_map, tl, normalize):
    nt = src.shape[0]
    const = lambda i: (0, 0)
    row = lambda i: (i, 0)
    hw = N_HEADS * HEAD_PAD
    out_shape = [jax.ShapeDtypeStruct((nt, KV_RANK), F32),
                 jax.ShapeDtypeStruct((nt, hw), MXU_DTYPE),
                 jax.ShapeDtypeStruct((nt, hw), MXU_DTYPE)]
    out_specs = [pl.BlockSpec((tl, KV_RANK), row),
                 pl.BlockSpec((tl, hw), row),
                 pl.BlockSpec((tl, hw), row)]
    n_in = 11
    kern = functools.partial(_kv_kernel, normalize=normalize)
    if not normalize:
        out_shape, out_specs = out_shape[1:], out_specs[1:]
        kern = lambda *refs: _kv_kernel(*refs[:n_in], None, *refs[n_in:], normalize=False)
    return pl.pallas_call(
        kern,
        grid=(nt // tl,),
        in_specs=[pl.BlockSpec((tl, KV_RANK), lambda i: (i, kv_blk)),
                  pl.BlockSpec((tl, HEAD_PAD), lambda i: (i, kr_blk)),
                  pl.BlockSpec((tl, HEAD_PAD), lambda i: (i, kr2_blk)),
                  pl.BlockSpec((1, KV_RANK), const),
                  pl.BlockSpec((KV_RANK, hw), const),
                  pl.BlockSpec((KV_RANK, hw), const),
                  pl.BlockSpec((1, hw), const),
                  pl.BlockSpec((1, HEAD_PAD), const),
                  pl.BlockSpec((1, HEAD_PAD), const),
                  pl.BlockSpec((tl, HEAD_PAD), table_map),
                  pl.BlockSpec((tl, HEAD_PAD), table_map)],
        out_specs=out_specs,
        out_shape=out_shape,
        compiler_params=_cp(("parallel",)),
        name="kv_proj" if normalize else "kv_proj_cache",
    )(src, kr_src, kr_src, norm_g, wk, wv, vone, kg, kg2, cos, sin)


def _q_kernel(qa_ref, ng_ref, wq_ref, wq2_ref, qg_ref, qg2_ref, cos_ref, sin_ref, q_ref):
    qn = (_rms(qa_ref[...].astype(F32), Q_RANK) * ng_ref[...]).astype(MXU_DTYPE)
    qfull = jnp.dot(qn, wq_ref[...], preferred_element_type=F32)
    qperm = jnp.dot(qn, wq2_ref[...], preferred_element_type=F32)
    cg = cos_ref[...] * qg_ref[...]
    sg = sin_ref[...] * qg2_ref[...]
    for h in range(N_HEADS):
        sl = slice(h * HEAD_PAD, (h + 1) * HEAD_PAD)
        q_ref[:, sl] = _head_norm_rope(qfull[:, sl], qperm[:, sl], cg, sg, SOFTMAX_C).astype(q_ref.dtype)


def _q_proj(proj, norm_g, wq, wq2, qg, qg2, cos, sin, table_map, tl):
    nt = proj.shape[0]
    const = lambda i: (0, 0)
    hw = N_HEADS * HEAD_PAD
    return pl.pallas_call(
        _q_kernel,
        grid=(nt // tl,),
        in_specs=[pl.BlockSpec((tl, Q_PAD), lambda i: (i, OFF_Q // Q_PAD)),
                  pl.BlockSpec((1, Q_PAD), const),
                  pl.BlockSpec((Q_PAD, hw), const),
                  pl.BlockSpec((Q_PAD, hw), const),
                  pl.BlockSpec((1, HEAD_PAD), const),
                  pl.BlockSpec((1, HEAD_PAD), const),
                  pl.BlockSpec((tl, HEAD_PAD), table_map),
                  pl.BlockSpec((tl, HEAD_PAD), table_map)],
        out_specs=pl.BlockSpec((tl, hw), lambda i: (i, 0)),
        out_shape=jax.ShapeDtypeStruct((nt, hw), MXU_DTYPE),
        compiler_params=_cp(("parallel",)),
        name="q_proj",
    )(proj, norm_g, wq, wq2, qg, qg2, cos, sin)


def _rope_tables(tl, dec_seq):
    a = ROPE_DIM // 2
    t = np.arange(dec_seq)
    pos = np.stack([t // GRID_W, t % GRID_W]).astype(np.float32)
    inv = np.power(np.float32(ROPE_BASE), -np.arange(0, a, 2, dtype=np.float32) / np.float32(a))
    ang = pos[:, :, None] * inv.astype(np.float32)
    cos = np.ones((tl + dec_seq, HEAD_PAD), np.float32)
    sin = np.zeros((tl + dec_seq, HEAD_PAD), np.float32)
    hq = a // 2
    for i in range(2):
        lo = NOPE_DIM + i * a
        cos[tl:, lo:lo + hq] = np.cos(ang[i])
        cos[tl:, lo + hq:lo + a] = np.cos(ang[i])
        sin[tl:, lo:lo + hq] = -np.sin(ang[i])
        sin[tl:, lo + hq:lo + a] = np.sin(ang[i])
    return jnp.asarray(cos), jnp.asarray(sin)


def _attn_kernel(*refs, has_ctx, chunk):
    if has_ctx:
        q_ref, k_ref, v_ref, kc_ref, vc_ref, o_ref = refs
    else:
        q_ref, k_ref, v_ref, o_ref = refs
    nt_dims = (((1,), (1,)), ((), ()))
    tq = q_ref.shape[0]
    length = k_ref.shape[0]
    chunks = [(k_ref, v_ref, j, min(chunk, length - j)) for j in range(0, length, chunk)]
    if has_ctx:
        chunks = [(kc_ref, vc_ref, 0, kc_ref.shape[0])] + chunks
    heads = [slice(h * HEAD_PAD, (h + 1) * HEAD_PAD) for h in range(2)]
    qs = [q_ref[:, hs] for hs in heads]
    m = [jnp.full((tq, 1), -jnp.inf, F32) for _ in heads]
    acc = [jnp.zeros((tq, HEAD_PAD), F32) for _ in heads]
    for kr, vr, start, size in chunks:
        rows = pl.ds(start, size)
        for h, hs in enumerate(heads):
            s = lax.dot_general(qs[h], kr[rows, hs], nt_dims, preferred_element_type=F32)
            m_new = jnp.maximum(m[h], jnp.max(s, axis=-1, keepdims=True))
            p = jnp.exp2(s - m_new).astype(MXU_DTYPE)
            acc[h] = acc[h] * jnp.exp2(m[h] - m_new) + jnp.dot(p, vr[rows, hs],
                                                               preferred_element_type=F32)
            m[h] = m_new
    outs = [a[:, 0:V_DIM] / a[:, V_DIM:V_DIM + 1] for a in acc]
    o_ref[...] = jnp.concatenate(outs, axis=-1).astype(o_ref.dtype)


def _attention(q, k, v, out_rows, row_off, bsz, length, tq, ctx=None):
    assert row_off % length == 0 and length % tq == 0
    qb0 = row_off // tq
    kb0 = row_off // length
    nq = length // tq
    hp = N_HEADS // 2
    in_specs = [pl.BlockSpec((tq, 2 * HEAD_PAD), lambda b, h, i: (qb0 + b * nq + i, h)),
                pl.BlockSpec((length, 2 * HEAD_PAD), lambda b, h, i: (kb0 + b, h)),
                pl.BlockSpec((length, 2 * HEAD_PAD), lambda b, h, i: (kb0 + b, h))]
    args = [q, k, v]
    if ctx is not None:
        kc, vc, past = ctx
        in_specs += [pl.BlockSpec((past, 2 * HEAD_PAD), lambda b, h, i: (b, h)),
                     pl.BlockSpec((past, 2 * HEAD_PAD), lambda b, h, i: (b, h))]
        args += [kc, vc]
    return pl.pallas_call(
        functools.partial(_attn_kernel, has_ctx=ctx is not None, chunk=min(length, 2048)),
        grid=(bsz, hp, nq),
        in_specs=in_specs,
        out_specs=pl.BlockSpec((tq, 2 * V_DIM), lambda b, h, i: (b * nq + i, h)),
        out_shape=jax.ShapeDtypeStruct((out_rows, N_HEADS * V_DIM), MXU_DTYPE),
        compiler_params=_cp(("parallel", "parallel", "arbitrary")),
        name="attention_ctx" if ctx is None else "attention_latent",
    )(*args)


def _merge_kernel(p_ref, h_ref, a_ref, g0_ref, g1_ref, g2_ref, x_ref, mod_ref,
                  wp_ref, wh_ref, wa_ref, wo_ref, o_ref):
    yp = jnp.dot(p_ref[...], wp_ref[...], preferred_element_type=F32)
    yh = jnp.dot(h_ref[...], wh_ref[...], preferred_element_type=F32)
    ya = jnp.dot(a_ref[...], wa_ref[...], preferred_element_type=F32)
    gate = lambda ref: _sigmoid(ref[...].astype(F32))
    merged = gate(g0_ref) * yp + gate(g1_ref) * yh + gate(g2_ref) * ya
    mix = jnp.dot(merged.astype(MXU_DTYPE), wo_ref[...], preferred_element_type=F32)
    o_ref[...] = x_ref[...] + mod_ref[0][2:3] * mix


def _merge(p, hy, att, proj, x, mod, wp, wh, wa, wo, tm, ctx_rows, dec_seq):
    nt, d = x.shape
    row = lambda i: (i, 0)
    const = lambda i: (0, 0)
    g0 = OFF_G // d
    return pl.pallas_call(
        _merge_kernel,
        grid=(nt // tm,),
        in_specs=[pl.BlockSpec((tm, POOL_WIDTH), row),
                  pl.BlockSpec((tm, HY_WIDTH), row),
                  pl.BlockSpec((tm, N_HEADS * V_DIM), row),
                  pl.BlockSpec((tm, d), lambda i: (i, g0)),
                  pl.BlockSpec((tm, d), lambda i: (i, g0 + 1)),
                  pl.BlockSpec((tm, d), lambda i: (i, g0 + 2)),
                  pl.BlockSpec((tm, d), row),
                  pl.BlockSpec((1, 6, d), _mod_row_map(tm, ctx_rows, dec_seq)),
                  pl.BlockSpec((POOL_WIDTH, d), const),
                  pl.BlockSpec((HY_WIDTH, d), const),
                  pl.BlockSpec((N_HEADS * V_DIM, d), const),
                  pl.BlockSpec((d, d), const)],
        out_specs=pl.BlockSpec((tm, d), row),
        out_shape=jax.ShapeDtypeStruct((nt, d), F32),
        compiler_params=_cp(("parallel",)),
        name="merge",
    )(p, hy, att, proj, proj, proj, x, mod, wp, wh, wa, wo)


def _norm2(x_ref, mod_ref, g_ref):
    m = mod_ref[0]
    return (_rms(x_ref[...], D_MODEL) * g_ref[...]) * (1.0 + m[4:5]) + m[3:4]


def _ffn_kernel(x_ref, mod_ref, g_ref, w1_ref, w3_ref, w2_ref, o_ref, h_ref, acc_ref):
    j = pl.program_id(1)

    @pl.when(j == 0)
    def _():
        h_ref[...] = _norm2(x_ref, mod_ref, g_ref).astype(h_ref.dtype)
        acc_ref[...] = jnp.zeros_like(acc_ref)

    h = h_ref[...]
    a = jnp.dot(h, w1_ref[...], preferred_element_type=F32)
    b = jnp.dot(h, w3_ref[...], preferred_element_type=F32)
    acc_ref[...] += jnp.dot((_silu(a) * b).astype(MXU_DTYPE), w2_ref[...], preferred_element_type=F32)

    @pl.when(j == pl.num_programs(1) - 1)
    def _():
        o_ref[...] = x_ref[...] + mod_ref[0][5:6] * acc_ref[...]


def _ffn_dense(x, mod, norm_g, w1, w3, w2, tm, tf, ctx_rows, dec_seq):
    nt, d = x.shape
    ff = w1.shape[1]
    return pl.pallas_call(
        _ffn_kernel,
        grid=(nt // tm, ff // tf),
        in_specs=[pl.BlockSpec((tm, d), lambda i, j: (i, 0)),
                  pl.BlockSpec((1, 6, d), _mod_row_map(tm, ctx_rows, dec_seq)),
                  pl.BlockSpec((1, d), lambda i, j: (0, 0)),
                  pl.BlockSpec((d, tf), lambda i, j: (0, j)),
                  pl.BlockSpec((d, tf), lambda i, j: (0, j)),
                  pl.BlockSpec((tf, d), lambda i, j: (j, 0))],
        out_specs=pl.BlockSpec((tm, d), lambda i, j: (i, 0)),
        out_shape=jax.ShapeDtypeStruct((nt, d), F32),
        scratch_shapes=[pltpu.VMEM((tm, d), MXU_DTYPE), pltpu.VMEM((tm, d), F32)],
        compiler_params=_cp(("parallel", "arbitrary")),
        name="ffn_dense",
    )(x, mod, norm_g, w1, w3, w2)


def _router_kernel(x_ref, mod_ref, g_ref, r_ref, h_ref, sel_ref):
    h = _norm2(x_ref, mod_ref, g_ref)
    h_ref[...] = h
    logits = _mm_f32(h, r_ref[...])
    lane = lax.broadcasted_iota(jnp.int32, logits.shape, 1).astype(F32)
    neg = jnp.float32(-jnp.inf)
    logits = jnp.where(lane < N_EXPERTS, logits, neg)
    m1 = jnp.max(logits, axis=-1, keepdims=True)
    i1 = jnp.min(jnp.where(logits == m1, lane, float(LANE)), axis=-1, keepdims=True)
    rest = jnp.where(lane == i1, neg, logits)
    m2 = jnp.max(rest, axis=-1, keepdims=True)
    i2 = jnp.min(jnp.where(rest == m2, lane, float(LANE)), axis=-1, keepdims=True)
    e2 = jnp.exp(m2 - m1)
    w_top = 1.0 / (1.0 + e2)
    sel_ref[...] = (jnp.where(lane == 0.0, i1, 0.0) + jnp.where(lane == 1.0, i2, 0.0)
                    + jnp.where(lane == 2.0, w_top, 0.0) + jnp.where(lane == 3.0, e2 * w_top, 0.0))


def _router(x, mod, norm_g, router, tm, ctx_rows, dec_seq):
    nt, d = x.shape
    rp = jnp.pad(router, ((0, 0), (0, LANE - N_EXPERTS)))
    return pl.pallas_call(
        _router_kernel,
        grid=(nt // tm,),
        in_specs=[pl.BlockSpec((tm, d), lambda i: (i, 0)),
                  pl.BlockSpec((1, 6, d), _mod_row_map(tm, ctx_rows, dec_seq)),
                  pl.BlockSpec((1, d), lambda i: (0, 0)),
                  pl.BlockSpec((d, LANE), lambda i: (0, 0))],
        out_specs=[pl.BlockSpec((tm, d), lambda i: (i, 0)),
                   pl.BlockSpec((tm, LANE), lambda i: (i, 0))],
        out_shape=[jax.ShapeDtypeStruct((nt, d), F32),
                   jax.ShapeDtypeStruct((nt, LANE), F32)],
        compiler_params=_cp(("parallel",)),
        name="router",
    )(x, mod, norm_g, rp)


def _dispatch_plan(sel, rows_per_tile):
    nt = sel.shape[0]
    n_pairs = 2 * nt
    r = rows_per_tile
    n_tiles = n_pairs // r + N_EXPERTS
    e_p = sel[:, 0:2].astype(jnp.int32).T.reshape(n_pairs)
    experts = jnp.arange(N_EXPERTS, dtype=jnp.int32)
    onehot = (e_p[:, None] == experts[None, :]).astype(jnp.int32)
    csum = jnp.cumsum(onehot, axis=0)
    rank = jnp.sum((csum - onehot) * onehot, axis=1)
    counts = csum[-1]
    padded = (counts + r - 1) // r * r
    ends = jnp.cumsum(padded)
    starts = ends - padded
    slot_p = jnp.sum(onehot * starts[None, :], axis=1) + rank
    tile_start = jnp.arange(n_tiles, dtype=jnp.int32) * r
    tile_expert = jnp.minimum(jnp.sum((tile_start[:, None] >= ends[None, :]).astype(jnp.int32), axis=1),
                              N_EXPERTS - 1)
    run_end = jnp.sum((tile_expert[:, None] == experts[None, :]) * (starts + counts)[None, :], axis=1)
    tile_rows = jnp.clip(run_end - tile_start, 0, r)
    return tile_expert, tile_rows, slot_p


def _row_copies(src, src_rows, dst, dst_rows, sem, n):
    return [pltpu.make_async_copy(src.at[pl.ds(src_rows(i), 1), :], dst.at[pl.ds(dst_rows(i), 1), :], sem)
            for i in range(n)]


def _dispatch_kernel(idx_ref, h_ref, xs_hbm, sem, *, n):
    copies = _row_copies(h_ref, lambda i: i, xs_hbm, lambda i: idx_ref[0, 0, i], sem, n)
    for cp in copies:
        cp.start()
    for cp in copies:
        cp.wait()


def _moe_dispatch(h, slot_p, n_slots, pb):
    nt, d = h.shape
    nb = nt // pb
    n_blocks = slot_p.shape[0] // pb
    return pl.pallas_call(
        functools.partial(_dispatch_kernel, n=pb),
        grid=(n_blocks,),
        in_specs=[pl.BlockSpec((1, 1, pb), lambda i: (i, 0, 0), memory_space=pltpu.SMEM),
                  pl.BlockSpec((pb, d), lambda i: (i % nb, 0))],
        out_specs=pl.BlockSpec(memory_space=pl.ANY),
        out_shape=jax.ShapeDtypeStruct((n_slots, d), F32),
        scratch_shapes=[pltpu.SemaphoreType.DMA(())],
        compiler_params=_cp(("arbitrary",)),
        name="moe_dispatch",
    )(slot_p.reshape(n_blocks, 1, pb), h)


def _moe_kernel(te_ref, tr_ref, xs_ref, w1_ref, w3_ref, w2_ref, ys_ref, *, rows, sub):
    n_rows = tr_ref[pl.program_id(0)]

    @pl.when(n_rows > 0)
    def _():
        for j in range(rows // sub):
            rs = pl.ds(j * sub, sub)
            occupied = j * sub + lax.broadcasted_iota(jnp.int32, (sub, 1), 0) < n_rows
            xb = jnp.where(occupied, xs_ref[rs, :], 0.0).astype(MXU_DTYPE)
            a = jnp.dot(xb, w1_ref[0], preferred_element_type=F32)
            b = jnp.dot(xb, w3_ref[0], preferred_element_type=F32)
            ys_ref[rs, :] = jnp.dot((_silu(a) * b).astype(MXU_DTYPE), w2_ref[0],
                                    preferred_element_type=F32)

    @pl.when(n_rows == 0)
    def _():
        ys_ref[...] = jnp.zeros_like(ys_ref)


def _ffn_moe(xs, tile_expert, tile_rows, w1, w3, w2, rows):
    n_slots, d = xs.shape
    _, _, ff = w1.shape
    wspec = lambda shape: pl.BlockSpec((1,) + shape, lambda t, te, tr: (te[t], 0, 0))
    grid_spec = pltpu.PrefetchScalarGridSpec(
        num_scalar_prefetch=2,
        grid=(n_slots // rows,),
        in_specs=[pl.BlockSpec((rows, d), lambda t, te, tr: (t, 0)),
                  wspec((d, ff)), wspec((d, ff)), wspec((ff, d))],
        out_specs=pl.BlockSpec((rows, d), lambda t, te, tr: (t, 0)))
    return pl.pallas_call(
        functools.partial(_moe_kernel, rows=rows, sub=min(rows, 256)),
        grid_spec=grid_spec,
        out_shape=jax.ShapeDtypeStruct((n_slots, d), F32),
        compiler_params=_cp(("arbitrary",)),
        name="ffn_moe",
    )(tile_expert, tile_rows, xs, w1, w3, w2)


def _combine_kernel(idx0_ref, idx1_ref, x_ref, mod_ref, sel_ref, ys_hbm, o_ref, buf, sem, *, n):
    copies = (_row_copies(ys_hbm, lambda i: idx0_ref[0, 0, i], buf.at[0], lambda i: i, sem, n)
              + _row_copies(ys_hbm, lambda i: idx1_ref[0, 0, i], buf.at[1], lambda i: i, sem, n))
    for cp in copies:
        cp.start()
    for cp in copies:
        cp.wait()
    sel = sel_ref[...]
    f = sel[:, 2:3] * buf[0] + sel[:, 3:4] * buf[1]
    o_ref[...] = x_ref[...] + mod_ref[0][5:6] * f


def _moe_combine(x, mod, sel, ys, slot_p, tm, ctx_rows, dec_seq):
    nt, d = x.shape
    nb = nt // tm
    idx = slot_p.reshape(2 * nb, 1, tm)
    row = lambda i: (i, 0)
    return pl.pallas_call(
        functools.partial(_combine_kernel, n=tm),
        grid=(nb,),
        in_specs=[pl.BlockSpec((1, 1, tm), lambda i: (i, 0, 0), memory_space=pltpu.SMEM),
                  pl.BlockSpec((1, 1, tm), lambda i: (nb + i, 0, 0), memory_space=pltpu.SMEM),
                  pl.BlockSpec((tm, d), row),
                  pl.BlockSpec((1, 6, d), _mod_row_map(tm, ctx_rows, dec_seq)),
                  pl.BlockSpec((tm, LANE), row),
                  pl.BlockSpec(memory_space=pl.ANY)],
        out_specs=pl.BlockSpec((tm, d), row),
        out_shape=jax.ShapeDtypeStruct((nt, d), F32),
        scratch_shapes=[pltpu.VMEM((2, tm, d), F32), pltpu.SemaphoreType.DMA(())],
        compiler_params=_cp(("arbitrary",)),
        name="moe_combine",
    )(idx, idx, x, mod, sel, ys)


def _prep_w_in(w_in):
    depth, d, _ = w_in.shape
    c1 = POOL_WIDTH
    c2 = c1 + 3 * HY_WIDTH
    c3 = c2 + Q_RANK
    c4 = c3 + KV_RANK
    c5 = c4 + ROPE_DIM
    z = lambda n: jnp.zeros((depth, d, n), w_in.dtype)
    kr_group = jnp.concatenate([z(NOPE_DIM), w_in[:, :, c4:c5], z(HEAD_PAD - QK_DIM)], axis=-1)
    cols = [w_in[:, :, c1:c2], w_in[:, :, 0:c1], w_in[:, :, c3:c4],
            kr_group, _partner_lanes(kr_group),
            w_in[:, :, c2:c3], z(Q_PAD - Q_RANK), w_in[:, :, c5:]]
    out = jnp.concatenate(cols, axis=-1).astype(MXU_DTYPE)
    assert out.shape[-1] == IN_PAD
    return out


def _head_pad_cols(w, width):
    lead = w.shape[:-1]
    w = w.reshape(*lead, N_HEADS, width)
    w = jnp.pad(w, [(0, 0)] * len(lead) + [(0, 0), (0, HEAD_PAD - width)])
    return w.reshape(*lead, N_HEADS * HEAD_PAD)


def _pad_last(a, n):
    return jnp.pad(a, [(0, 0)] * (a.ndim - 1) + [(0, n - a.shape[-1])])


def kernel(x_prompt, x_sample, cache_ckv, cache_krope, c, c_ctx, w_mod, b_mod, norm1_g, norm2_g, w_in, pool_w, pool_scale, pool_out, hy_conv_w, hy_conv_b, hy_w1, hy_b1, hy_w2, hy_b2, hy_w3, hy_freq, hy_decay, hy_bias, hy_out, q_norm_g, w_qb, kv_norm_g, w_kvb, qk_q_g, qk_k_g, mla_out, w_out, ffn_w1, ffn_w3, ffn_w2, moe_router, moe_w1, moe_w3, moe_w2):
    batch, seq, d = x_prompt.shape
    dec_batch, dec_seq, _ = x_sample.shape
    depth = w_mod.shape[0]
    past = cache_ckv.shape[2]
    ctx_rows = batch * seq
    lat_rows = dec_batch * dec_seq
    nt = ctx_rows + lat_rows
    tl = 256
    assert seq % tl == 0 and dec_seq % tl == 0 and ctx_rows % dec_seq == 0 and past % tl == 0
    tm = min(1024, math.gcd(ctx_rows, dec_seq))
    n_ctx_tiles = ctx_rows // tl
    bf = lambda a: a.astype(MXU_DTYPE)

    w_in_p = _prep_w_in(w_in)
    eye = jnp.eye(POOL_GROUPS, dtype=pool_w.dtype)
    gc = POOL_WIDTH // POOL_GROUPS
    pool_bd = bf(jnp.einsum('lgcd,gh->lgchd', pool_w, eye).reshape(depth, POOL_WIDTH, POOL_WIDTH))
    kvb = w_kvb.reshape(depth, KV_RANK, N_HEADS, NOPE_DIM + V_DIM)
    wk = bf(_head_pad_cols(kvb[..., :NOPE_DIM].reshape(depth, KV_RANK, N_HEADS * NOPE_DIM), NOPE_DIM))
    wv = bf(_head_pad_cols(kvb[..., NOPE_DIM:].reshape(depth, KV_RANK, N_HEADS * V_DIM), V_DIM))
    vone = jnp.tile((jnp.arange(HEAD_PAD) == V_DIM).astype(F32), N_HEADS)[None, :]
    wq_f = jnp.pad(_head_pad_cols(w_qb, QK_DIM), ((0, 0), (0, Q_PAD - Q_RANK), (0, 0)))
    wq, wq2 = bf(wq_f), bf(_partner_lanes(wq_f))
    q_norm_p = _pad_last(q_norm_g, Q_PAD)
    qg = _pad_last(qk_q_g, HEAD_PAD)
    kg = _pad_last(qk_k_g, HEAD_PAD)
    qg2, kg2 = _partner_lanes(qg), _partner_lanes(kg)
    pool_out_b, hy_out_b, mla_out_b, w_out_b = bf(pool_out), bf(hy_out), bf(mla_out), bf(w_out)
    ffn_w1_b, ffn_w3_b, ffn_w2_b = bf(ffn_w1), bf(ffn_w3), bf(ffn_w2)
    moe_w1_b, moe_w3_b, moe_w2_b = bf(moe_w1), bf(moe_w3), bf(moe_w2)

    tp = math.gcd(512, math.gcd(ctx_rows, math.gcd(dec_seq, dec_batch * past)))
    cos_t, sin_t = _rope_tables(tp, dec_seq)
    n_ctx_tp = ctx_rows // tp
    n_lat_tp = dec_seq // tp
    table_map = lambda i: (jnp.where(i < n_ctx_tp, 0, 1 + (i - n_ctx_tp) % n_lat_tp), 0)
    ident_map = lambda i: (0, 0)
    mats_ctx = _dft_matrices(seq)
    mats_lat = _dft_matrices(dec_seq)

    n_cond = 1 + dec_batch
    cond_rows = -(-n_cond // SUBLANE) * SUBLANE
    cond = jnp.concatenate([c_ctx[None, :], c, jnp.zeros((cond_rows - n_cond, d), c.dtype)], axis=0)
    mod_all = _modulation(cond, w_mod, b_mod)[:, :n_cond].reshape(depth, n_cond, 6, d)
    filt_ctx = _hyena_filters(seq, hy_w1, hy_b1, hy_w2, hy_b2, hy_w3, hy_freq, hy_decay)
    filt_lat = _hyena_filters(dec_seq, hy_w1, hy_b1, hy_w2, hy_b2, hy_w3, hy_freq, hy_decay)
    cache_kr_p = jnp.pad(cache_krope, ((0, 0), (0, 0), (0, 0), (NOPE_DIM, HEAD_PAD - QK_DIM)))
    cache_kr_p = jnp.concatenate([cache_kr_p, _partner_lanes(cache_kr_p)], axis=-1)

    x = jnp.concatenate([x_prompt.reshape(ctx_rows, d), x_sample.reshape(lat_rows, d)], axis=0)
    ckv_list, kr_list = [], []
    for i in range(depth):
        j = i // 2
        mod = mod_all[i]
        r1 = lambda a: a[i].reshape(1, -1)
        proj = _inproj(x, mod, r1(norm1_g), w_in_p[i], tm, ctx_rows, dec_seq)

        vx, vxb, x0, pooled = _local_mixers(proj, hy_conv_w[i], r1(hy_conv_b), pool_bd[i], r1(pool_scale),
                                            tl, n_ctx_tiles, seq, dec_seq)
        bias = r1(hy_bias)
        hy_c = _long_conv(vx[:ctx_rows], vxb[:ctx_rows], x0[:ctx_rows], filt_ctx[i], bias,
                          mats_ctx, batch, seq)
        hy_l = _long_conv(vx[ctx_rows:], vxb[ctx_rows:], x0[ctx_rows:], filt_lat[i], bias,
                          mats_lat, dec_batch, dec_seq)
        hy = jnp.concatenate([hy_c, hy_l], axis=0)

        ckv, k, v = _kv_proj(proj, OFF_KV // KV_RANK, proj, OFF_KROPE // HEAD_PAD, OFF_KROPE2 // HEAD_PAD,
                             r1(kv_norm_g), wk[i], wv[i], vone, r1(kg), r1(kg2), cos_t, sin_t,
                             table_map, tp, True)
        kc, vc = _kv_proj(cache_ckv[:, i].reshape(dec_batch * past, KV_RANK), 0,
                          cache_kr_p[:, i].reshape(dec_batch * past, 2 * HEAD_PAD), 0, 1,
                          r1(kv_norm_g), wk[i], wv[i], vone, r1(kg), r1(kg2), cos_t, sin_t,
                          ident_map, tp, False)
        q = _q_proj(proj, r1(q_norm_p), wq[i], wq2[i], r1(qg), r1(qg2), cos_t, sin_t, table_map, tp)
        att_c = _attention(q, k, v, ctx_rows, 0, batch, seq, min(seq, 256))
        att_l = _attention(q, k, v, lat_rows, ctx_rows, dec_batch, dec_seq, min(dec_seq, 512),
                           ctx=(kc, vc, past))
        att = jnp.concatenate([att_c, att_l], axis=0)

        x = _merge(pooled, hy, att, proj, x, mod, pool_out_b[i], hy_out_b[i], mla_out_b[i], w_out_b[i],
                   min(tm, 512), ctx_rows, dec_seq)

        if i % 2 == 1:
            h2, sel = _router(x, mod, r1(norm2_g), moe_router[j], tm, ctx_rows, dec_seq)
            tile_expert, tile_rows, slot_p = _dispatch_plan(sel, MOE_ROWS)
            xs = _moe_dispatch(h2, slot_p, tile_expert.shape[0] * MOE_ROWS, min(tm, 512))
            ys = _ffn_moe(xs, tile_expert, tile_rows, moe_w1_b[j], moe_w3_b[j], moe_w2_b[j], MOE_ROWS)
            x = _moe_combine(x, mod, sel, ys, slot_p, min(tm, 512), ctx_rows, dec_seq)
        else:
            x = _ffn_dense(x, mod, r1(norm2_g), ffn_w1_b[j], ffn_w3_b[j], ffn_w2_b[j],
                           tm, ffn_w1.shape[-1] // 2, ctx_rows, dec_seq)

        ckv_list.append(ckv[:ctx_rows].reshape(batch, seq, KV_RANK))
        kr_list.append(proj[:ctx_rows, OFF_KROPE + NOPE_DIM:OFF_KROPE + QK_DIM].astype(F32)
                       .reshape(batch, seq, ROPE_DIM))

    y_p = x[:ctx_rows].reshape(batch, seq, d)
    y_s = x[ctx_rows:].reshape(dec_batch, dec_seq, d)
    return (y_p, y_s, jnp.stack(ckv_list, axis=1), jnp.stack(kr_list, axis=1))
```

```python
import functools
import math

import numpy as np
import jax
import jax.numpy as jnp
from jax import lax
from jax.experimental import pallas as pl
from jax.experimental.pallas import tpu as pltpu

F32 = jnp.float32
MXU_DTYPE = jnp.bfloat16
HIGHEST = lax.Precision.HIGHEST

D_MODEL = 1024
GRID_W = 64
EPS = 1e-6
POOL_WIDTH = 256
POOL_GROUPS = 4
POOL_WINDOWS = (2, 4, 8, 16)
HY_WIDTH = 256
HY_SHORT = 3
HY_BANDS = 8
HY_EMB = 1 + 2 * HY_BANDS
HY_FFN = 64
N_HEADS = 8
Q_RANK = 384
KV_RANK = 256
NOPE_DIM = 64
ROPE_DIM = 32
V_DIM = 64
QK_DIM = NOPE_DIM + ROPE_DIM
ROPE_BASE = 10000.0
N_EXPERTS = 8

LANE = 128
SUBLANE = 8
HALO = 16
HEAD_PAD = LANE
Q_PAD = 512
MOE_ROWS = 512
VMEM_LIMIT = 56 * 1024 * 1024
SOFTMAX_C = (QK_DIM ** -0.5) * math.log2(math.e)

OFF_HY = 0
OFF_POOL = 3 * HY_WIDTH
OFF_KV = OFF_POOL + POOL_WIDTH
OFF_KROPE = OFF_KV + KV_RANK
OFF_KROPE2 = OFF_KROPE + HEAD_PAD
OFF_Q = 1536
OFF_G = OFF_Q + Q_PAD
IN_PAD = OFF_G + 3 * D_MODEL


def _cp(sem, vmem=VMEM_LIMIT):
    return pltpu.CompilerParams(dimension_semantics=sem, vmem_limit_bytes=vmem)


def _mm(a, b):
    return jnp.dot(a.astype(MXU_DTYPE), b.astype(MXU_DTYPE), preferred_element_type=F32)


def _mm_f32(a, b):
    return jnp.dot(a, b, preferred_element_type=F32, precision=HIGHEST)


def _sigmoid(x):
    return 1.0 / (1.0 + jnp.exp(-x))


def _silu(x):
    return x * _sigmoid(x)


def _rms(x, n):
    ms = jnp.sum(x * x, axis=-1, keepdims=True) * (1.0 / n)
    return x * lax.rsqrt(ms + EPS)


def _mod_row_map(tm, ctx_rows, dec_seq):
    def index_map(i, *_):
        row0 = i * tm
        return (jnp.where(row0 < ctx_rows, 0, 1 + (row0 - ctx_rows) // dec_seq), 0, 0)
    return index_map


def _mod_kernel(c_ref, w_ref, b_ref, o_ref):
    o_ref[0] = _mm_f32(_silu(c_ref[...]), w_ref[0]) + b_ref[0]


def _modulation(cond, w_mod, b_mod):
    depth, d, n6 = w_mod.shape
    r = cond.shape[0]
    tn = 1024
    return pl.pallas_call(
        _mod_kernel,
        grid=(depth, n6 // tn),
        in_specs=[pl.BlockSpec((r, d), lambda l, j: (0, 0)),
                  pl.BlockSpec((1, d, tn), lambda l, j: (l, 0, j)),
                  pl.BlockSpec((1, 1, tn), lambda l, j: (l, 0, j))],
        out_specs=pl.BlockSpec((1, r, tn), lambda l, j: (l, 0, j)),
        out_shape=jax.ShapeDtypeStruct((depth, r, n6), F32),
        compiler_params=_cp(("parallel", "parallel")),
        name="modulation",
    )(cond, w_mod, b_mod.reshape(depth, 1, n6))


def _inproj_kernel(x_ref, mod_ref, g_ref, w_ref, o_ref, h_ref):
    @pl.when(pl.program_id(1) == 0)
    def _():
        m = mod_ref[0]
        h = _rms(x_ref[...], D_MODEL) * g_ref[...]
        h_ref[...] = (h * (1.0 + m[1:2]) + m[0:1]).astype(h_ref.dtype)
    o_ref[...] = jnp.dot(h_ref[...], w_ref[...], preferred_element_type=F32).astype(o_ref.dtype)


def _inproj(x, mod, norm_g, w, tm, ctx_rows, dec_seq):
    nt, d = x.shape
    n = w.shape[1]
    tn = 1024
    return pl.pallas_call(
        _inproj_kernel,
        grid=(nt // tm, n // tn),
        in_specs=[pl.BlockSpec((tm, d), lambda i, j: (i, 0)),
                  pl.BlockSpec((1, 6, d), _mod_row_map(tm, ctx_rows, dec_seq)),
                  pl.BlockSpec((1, d), lambda i, j: (0, 0)),
                  pl.BlockSpec((d, tn), lambda i, j: (0, j))],
        out_specs=pl.BlockSpec((tm, tn), lambda i, j: (i, j)),
        out_shape=jax.ShapeDtypeStruct((nt, n), MXU_DTYPE),
        scratch_shapes=[pltpu.VMEM((tm, d), MXU_DTYPE)],
        compiler_params=_cp(("parallel", "arbitrary")),
        name="inproj",
    )(x, mod, norm_g, w)


def _local_kernel(u_ref, prev_ref, next_ref, cw_ref, cb_ref, pw_ref, ps_ref,
                  vx_ref, vxb_ref, x0_ref, p_ref, ext_ref, *, tl, n_ctx_tiles, seq, dec_seq):
    i = pl.program_id(0)
    row0 = i * tl
    ctx = i < n_ctx_tiles
    p0 = jnp.where(ctx, row0 % seq, (row0 - n_ctx_tiles * tl) % dec_seq)
    length = jnp.where(ctx, seq, dec_seq)
    first = p0 == 0
    last = p0 + tl == length

    ext_ref[0:SUBLANE, :] = jnp.where(first, 0.0, prev_ref[HALO - SUBLANE:HALO, :].astype(F32))
    ext_ref[SUBLANE:SUBLANE + tl, :] = u_ref[...].astype(F32)
    ext_ref[SUBLANE + tl:, :] = jnp.where(last, 0.0, next_ref[0:SUBLANE, :].astype(F32))

    hw = 3 * HY_WIDTH
    cw = cw_ref[...]
    z = (cb_ref[...]
         + ext_ref[pl.ds(SUBLANE - 1, tl), 0:hw] * cw[0:1]
         + ext_ref[pl.ds(SUBLANE, tl), 0:hw] * cw[1:2]
         + ext_ref[pl.ds(SUBLANE + 1, tl), 0:hw] * cw[2:3])
    x0 = z[:, 0:HY_WIDTH]
    vx = z[:, 2 * HY_WIDTH:3 * HY_WIDTH] * z[:, HY_WIDTH:2 * HY_WIDTH]
    vx_ref[...] = vx
    vxb_ref[...] = vx.astype(vxb_ref.dtype)
    x0_ref[...] = x0

    def e(j):
        return ext_ref[pl.ds(SUBLANE + j, tl), hw:hw + POOL_WIDTH]

    u = e(0)
    s2 = e(-1) + u
    s4 = s2 + e(-2) + e(1)
    s8 = s4 + e(-4) + e(-3) + e(2) + e(3)
    s16 = s8 + e(-8) + e(-7) + e(-6) + e(-5) + e(4) + e(5) + e(6) + e(7)
    lane = lax.broadcasted_iota(jnp.int32, (tl, POOL_WIDTH), 1)
    gc = POOL_WIDTH // POOL_GROUPS
    ssum = jnp.where(lane < gc, s2, jnp.where(lane < 2 * gc, s4, jnp.where(lane < 3 * gc, s8, s16)))
    half = jnp.where(lane < gc, 1, jnp.where(lane < 2 * gc, 2, jnp.where(lane < 3 * gc, 4, 8)))
    t = p0 + lax.broadcasted_iota(jnp.int32, (tl, POOL_WIDTH), 0)
    cnt = jnp.minimum(t + half, length) - jnp.maximum(t - half, 0)
    pooled = ssum / cnt.astype(F32) - u
    p_ref[...] = (_mm(pooled, pw_ref[...]) * ps_ref[...]).astype(p_ref.dtype)


def _local_mixers(proj, conv_w, conv_b, pool_bd, pool_scale, tl, n_ctx_tiles, seq, dec_seq):
    nt = proj.shape[0]
    width = 3 * HY_WIDTH + POOL_WIDTH
    nb = tl // HALO
    last_blk = nt // HALO - 1
    kern = functools.partial(_local_kernel, tl=tl, n_ctx_tiles=n_ctx_tiles, seq=seq, dec_seq=dec_seq)
    row = lambda i: (i, 0)
    const = lambda i: (0, 0)
    return pl.pallas_call(
        kern,
        grid=(nt // tl,),
        in_specs=[pl.BlockSpec((tl, width), row),
                  pl.BlockSpec((HALO, width), lambda i: (jnp.maximum(i * nb - 1, 0), 0)),
                  pl.BlockSpec((HALO, width), lambda i: (jnp.minimum((i + 1) * nb, last_blk), 0)),
                  pl.BlockSpec((HY_SHORT, 3 * HY_WIDTH), const),
                  pl.BlockSpec((1, 3 * HY_WIDTH), const),
                  pl.BlockSpec((POOL_WIDTH, POOL_WIDTH), const),
                  pl.BlockSpec((1, POOL_WIDTH), const)],
        out_specs=[pl.BlockSpec((tl, HY_WIDTH), row)] * 3 + [pl.BlockSpec((tl, POOL_WIDTH), row)],
        out_shape=[jax.ShapeDtypeStruct((nt, HY_WIDTH), F32),
                   jax.ShapeDtypeStruct((nt, HY_WIDTH), MXU_DTYPE),
                   jax.ShapeDtypeStruct((nt, HY_WIDTH), F32),
                   jax.ShapeDtypeStruct((nt, POOL_WIDTH), MXU_DTYPE)],
        scratch_shapes=[pltpu.VMEM((tl + 2 * SUBLANE, width), F32)],
        compiler_params=_cp(("parallel",)),
        name="local_mixers",
    )(proj, proj, proj, conv_w, conv_b, pool_bd, pool_scale)


def _filter_kernel(z_ref, w1_ref, b1_ref, w2_ref, b2_ref, w3_ref, fr_ref, dec_ref, o_ref):
    z = z_ref[...]
    fr = fr_ref[0]
    hdn = jnp.sin(fr * (_mm_f32(z, w1_ref[0]) + b1_ref[0]))
    hdn = jnp.sin(fr * (_mm_f32(hdn, w2_ref[0]) + b2_ref[0]))
    filt = _mm_f32(hdn, w3_ref[0])
    win = jnp.exp(-z[:, 0:1] * jnp.abs(dec_ref[0]))
    o_ref[0] = filt * jnp.concatenate([win, win], axis=-1)


def _hyena_filters(length, w1, b1, w2, b2, w3, freq, decay):
    depth = w1.shape[0]
    t = np.linspace(0.0, 1.0, length, dtype=np.float32)[:, None]
    wpos = (np.float32(2.0 * math.pi / length) * np.arange(length, dtype=np.float32))[:, None]
    bands = np.linspace(1e-4, HY_BANDS - 1, HY_BANDS, dtype=np.float32)[None, :]
    emb_pad = 32
    z = np.zeros((length, emb_pad), np.float32)
    z[:, 0:1] = t
    z[:, 1:1 + HY_BANDS] = np.cos(bands * wpos)
    z[:, 1 + HY_BANDS:HY_EMB] = -np.sin(bands * wpos)
    w1p = jnp.pad(w1, ((0, 0), (0, emb_pad - HY_EMB), (0, 0)))
    tl = min(length, 512)
    lay = lambda l, i: (l, 0, 0)
    r3 = lambda a: a.reshape(depth, 1, a.shape[-1])
    return pl.pallas_call(
        _filter_kernel,
        grid=(depth, length // tl),
        in_specs=[pl.BlockSpec((tl, emb_pad), lambda l, i: (i, 0)),
                  pl.BlockSpec((1, emb_pad, HY_FFN), lay),
                  pl.BlockSpec((1, 1, HY_FFN), lay),
                  pl.BlockSpec((1, HY_FFN, HY_FFN), lay),
                  pl.BlockSpec((1, 1, HY_FFN), lay),
                  pl.BlockSpec((1, HY_FFN, 2 * HY_WIDTH), lay),
                  pl.BlockSpec((1, 1, HY_FFN), lay),
                  pl.BlockSpec((1, 1, HY_WIDTH), lay)],
        out_specs=pl.BlockSpec((1, tl, 2 * HY_WIDTH), lambda l, i: (l, i, 0)),
        out_shape=jax.ShapeDtypeStruct((depth, length, 2 * HY_WIDTH), F32),
        compiler_params=_cp(("parallel", "parallel")),
        name="hyena_filters",
    )(jnp.asarray(z), w1p, r3(b1), w2, r3(b2), w3, r3(freq), r3(decay))


def _dft_matrices(length):
    n = 2 * length
    f = jnp.arange(length, dtype=jnp.int32)[:, None]
    s = jnp.arange(length, dtype=jnp.int32)[None, :]
    blk = 64
    theta = np.float32(2.0 * math.pi / n)
    a_hi = ((f * (s[:, 0:length // blk] * blk)) % n).astype(F32) * theta
    a_lo = ((f * s[:, 0:blk]) % n).astype(F32) * theta
    c_hi, s_hi = jnp.cos(a_hi)[:, :, None], jnp.sin(a_hi)[:, :, None]
    c_lo, s_lo = jnp.cos(a_lo)[:, None, :], jnp.sin(a_lo)[:, None, :]
    cos = (c_hi * c_lo - s_hi * s_lo).reshape(length, length)
    sin = (s_hi * c_lo + c_hi * s_lo).reshape(length, length)
    alt = jnp.where(jnp.arange(length) % 2 == 0, 1.0, -1.0).astype(F32)
    fwd_im = jnp.where(f == 0, alt[None, :], -sin)
    fwd = jnp.concatenate([cos, fwd_im], axis=0)
    inv_re = jnp.where(s == 0, 1.0 / n, (2.0 / n) * cos)
    inv_im = jnp.where(s == 0, alt[:, None] / n, (-2.0 / n) * sin)
    inv = jnp.concatenate([inv_re, inv_im], axis=1)
    return fwd.astype(MXU_DTYPE), inv.astype(MXU_DTYPE)


def _dft_fwd_kernel(g_ref, x_ref, o_ref, *, bg):
    @pl.when(pl.program_id(2) == 0)
    def _():
        o_ref[...] = jnp.zeros_like(o_ref)
    g = g_ref[...]
    for b in range(bg):
        o_ref[b] += jnp.dot(g, x_ref[b], preferred_element_type=F32)


def _dft_fwd(gmat, x, bg):
    bsz, length, cols = x.shape
    tm = min(2 * length, 1024)
    tk = min(length, 512)
    return pl.pallas_call(
        functools.partial(_dft_fwd_kernel, bg=bg),
        grid=(bsz // bg, 2 * length // tm, length // tk),
        in_specs=[pl.BlockSpec((tm, tk), lambda g, m, k: (m, k)),
                  pl.BlockSpec((bg, tk, cols), lambda g, m, k: (g, k, 0))],
        out_specs=pl.BlockSpec((bg, tm, cols), lambda g, m, k: (g, m, 0)),
        out_shape=jax.ShapeDtypeStruct((bsz, 2 * length, cols), F32),
        compiler_params=_cp(("parallel", "parallel", "arbitrary")),
        name="dft_fwd",
    )(gmat, x)


def _dft_inv_kernel(gc_ref, gs_ref, xr_ref, xi_ref, ar_ref, ai_ref, hb0_ref, vx_ref, x0_ref,
                    bias_ref, o_ref, acc_ref, *, bg, tk):
    k = pl.program_id(2)

    @pl.when(k == 0)
    def _():
        acc_ref[...] = jnp.zeros_like(acc_ref)

    c = HY_WIDTH
    ar = ar_ref[...]
    ai = ai_ref[...]
    hb0 = hb0_ref[...]
    f0 = (k * tk + lax.broadcasted_iota(jnp.int32, (tk, c), 0)) == 0
    kr = ar[:, 0:c] + ar[:, c:2 * c] - hb0
    ki = jnp.where(f0, ai[:, 0:c] + ai[:, c:2 * c] - hb0, ai[:, 0:c] - ai[:, c:2 * c])
    gc = gc_ref[...]
    gs = gs_ref[...]
    for b in range(bg):
        xr = xr_ref[b]
        xi = xi_ref[b]
        zr = jnp.where(f0, xr * kr, xr * kr - xi * ki)
        zi = jnp.where(f0, xi * ki, xr * ki + xi * kr)
        acc_ref[b] += (jnp.dot(gc, zr.astype(MXU_DTYPE), preferred_element_type=F32)
                       + jnp.dot(gs, zi.astype(MXU_DTYPE), preferred_element_type=F32))

    @pl.when(k == pl.num_programs(2) - 1)
    def _():
        y = (acc_ref[...] + vx_ref[...] * bias_ref[...]) * x0_ref[...]
        o_ref[...] = y.astype(o_ref.dtype)


def _dft_inv(ginv, xspec, fspec, hb0, vx, x0, bias, bg):
    bsz, length, c = vx.shape
    tm = min(length, 512)
    tk = min(length, 512)
    nk = length // tk
    return pl.pallas_call(
        functools.partial(_dft_inv_kernel, bg=bg, tk=tk),
        grid=(bsz // bg, length // tm, nk),
        in_specs=[pl.BlockSpec((tm, tk), lambda g, m, k: (m, k)),
                  pl.BlockSpec((tm, tk), lambda g, m, k: (m, nk + k)),
                  pl.BlockSpec((bg, tk, c), lambda g, m, k: (g, k, 0)),
                  pl.BlockSpec((bg, tk, c), lambda g, m, k: (g, nk + k, 0)),
                  pl.BlockSpec((tk, 2 * c), lambda g, m, k: (k, 0)),
                  pl.BlockSpec((tk, 2 * c), lambda g, m, k: (nk + k, 0)),
                  pl.BlockSpec((1, c), lambda g, m, k: (0, 0)),
                  pl.BlockSpec((bg, tm, c), lambda g, m, k: (g, m, 0)),
                  pl.BlockSpec((bg, tm, c), lambda g, m, k: (g, m, 0)),
                  pl.BlockSpec((1, c), lambda g, m, k: (0, 0))],
        out_specs=pl.BlockSpec((bg, tm, c), lambda g, m, k: (g, m, 0)),
        out_shape=jax.ShapeDtypeStruct((bsz, length, c), MXU_DTYPE),
        scratch_shapes=[pltpu.VMEM((bg, tm, c), F32)],
        compiler_params=_cp(("parallel", "parallel", "arbitrary")),
        name="dft_inv",
    )(ginv, ginv, xspec, xspec, fspec, fspec, hb0, vx, x0, bias)


def _long_conv(vx, vxb, x0, filt, bias, mats, bsz, length):
    gfwd, ginv = mats
    c = vx.shape[-1]
    bg = math.gcd(bsz, 8)
    fspec = _dft_fwd(gfwd, filt.astype(MXU_DTYPE)[None], 1)[0]
    xspec = _dft_fwd(gfwd, vxb.reshape(bsz, length, c), bg)
    hb0 = filt[0:1, c:2 * c]
    out = _dft_inv(ginv, xspec, fspec, hb0, vx.reshape(bsz, length, c), x0.reshape(bsz, length, c),
                   bias, bg)
    return out.reshape(bsz * length, c)


def _rope_partner(r):
    a = ROPE_DIM // 2
    return a * (r // a) + (r % a + a // 2) % a


def _partner_lanes(a):
    idx = np.arange(a.shape[-1])
    r = idx % HEAD_PAD - NOPE_DIM
    rot = (r >= 0) & (r < ROPE_DIM)
    src = np.where(rot, idx - r + _rope_partner(np.clip(r, 0, ROPE_DIM - 1)), idx)
    return jnp.where(jnp.asarray(rot), a[..., src], jnp.zeros((), a.dtype))


def _head_norm_rope(x, xp, cg, sg, out_scale):
    ms = jnp.sum(x * x, axis=-1, keepdims=True) * (1.0 / QK_DIM)
    return (x * cg + xp * sg) * (lax.rsqrt(ms + EPS) * out_scale)


def _kv_kernel(kva_ref, kr_ref, kr2_ref, ng_ref, wk_ref, wv_ref, vone_ref, kg_ref, kg2_ref,
               cos_ref, sin_ref, ckv_ref, k_ref, v_ref, *, normalize):
    kva = kva_ref[...].astype(F32)
    ckv = _rms(kva, KV_RANK) * ng_ref[...] if normalize else kva
    if normalize:
        ckv_ref[...] = ckv
    cb = ckv.astype(MXU_DTYPE)
    kfull = jnp.dot(cb, wk_ref[...], preferred_element_type=F32)
    kr = kr_ref[...].astype(F32)
    cg = cos_ref[...] * kg_ref[...]
    sg = sin_ref[...] * kg2_ref[...]
    xp = kr2_ref[...].astype(F32)
    for h in range(N_HEADS):
        sl = slice(h * HEAD_PAD, (h + 1) * HEAD_PAD)
        k_ref[:, sl] = _head_norm_rope(kfull[:, sl] + kr, xp, cg, sg, 1.0).astype(k_ref.dtype)
    v = jnp.dot(cb, wv_ref[...], preferred_element_type=F32) + vone_ref[...]
    v_ref[...] = v.astype(v_ref.dtype)


def _kv_proj(src, kv_blk, kr_src, kr_blk, kr2_blk, norm_g, wk, wv, vone, kg, kg2, cos, sin,
             table_map, tl, normalize):
    nt = src.shape[0]
    const = lambda i: (0, 0)
    row = lambda i: (i, 0)
    hw = N_HEADS * HEAD_PAD
    out_shape = [jax.ShapeDtypeStruct((nt, KV_RANK), F32),
                 jax.ShapeDtypeStruct((nt, hw), MXU_DTYPE),
                 jax.ShapeDtypeStruct((nt, hw), MXU_DTYPE)]
    out_specs = [pl.BlockSpec((tl, KV_RANK), row),
                 pl.BlockSpec((tl, hw), row),
                 pl.BlockSpec((tl, hw), row)]
    n_in = 11
    kern = functools.partial(_kv_kernel, normalize=normalize)
    if not normalize:
        out_shape, out_specs = out_shape[1:], out_specs[1:]
        kern = lambda *refs: _kv_kernel(*refs[:n_in], None, *refs[n_in:], normalize=False)
    return pl.pallas_call(
        kern,
        grid=(nt // tl,),
        in_specs=[pl.BlockSpec((tl, KV_RANK), lambda i: (i, kv_blk)),
                  pl.BlockSpec((tl, HEAD_PAD), lambda i: (i, kr_blk)),
                  pl.BlockSpec((tl, HEAD_PAD), lambda i: (i, kr2_blk)),
                  pl.BlockSpec((1, KV_RANK), const),
                  pl.BlockSpec((KV_RANK, hw), const),
                  pl.BlockSpec((KV_RANK, hw), const),
                  pl.BlockSpec((1, hw), const),
                  pl.BlockSpec((1, HEAD_PAD), const),
                  pl.BlockSpec((1, HEAD_PAD), const),
                  pl.BlockSpec((tl, HEAD_PAD), table_map),
                  pl.BlockSpec((tl, HEAD_PAD), table_map)],
        out_specs=out_specs,
        out_shape=out_shape,
        compiler_params=_cp(("parallel",)),
        name="kv_proj" if normalize else "kv_proj_cache",
    )(src, kr_src, kr_src, norm_g, wk, wv, vone, kg, kg2, cos, sin)


def _q_kernel(qa_ref, ng_ref, wq_ref, wq2_ref, qg_ref, qg2_ref, cos_ref, sin_ref, q_ref):
    qn = (_rms(qa_ref[...].astype(F32), Q_RANK) * ng_ref[...]).astype(MXU_DTYPE)
    qfull = jnp.dot(qn, wq_ref[...], preferred_element_type=F32)
    qperm = jnp.dot(qn, wq2_ref[...], preferred_element_type=F32)
    cg = cos_ref[...] * qg_ref[...]
    sg = sin_ref[...] * qg2_ref[...]
    for h in range(N_HEADS):
        sl = slice(h * HEAD_PAD, (h + 1) * HEAD_PAD)
        q_ref[:, sl] = _head_norm_rope(qfull[:, sl], qperm[:, sl], cg, sg, SOFTMAX_C).astype(q_ref.dtype)


def _q_proj(proj, norm_g, wq, wq2, qg, qg2, cos, sin, table_map, tl):
    nt = proj.shape[0]
    const = lambda i: (0, 0)
    hw = N_HEADS * HEAD_PAD
    return pl.pallas_call(
        _q_kernel,
        grid=(nt // tl,),
        in_specs=[pl.BlockSpec((tl, Q_PAD), lambda i: (i, OFF_Q // Q_PAD)),
                  pl.BlockSpec((1, Q_PAD), const),
                  pl.BlockSpec((Q_PAD, hw), const),
                  pl.BlockSpec((Q_PAD, hw), const),
                  pl.BlockSpec((1, HEAD_PAD), const),
                  pl.BlockSpec((1, HEAD_PAD), const),
                  pl.BlockSpec((tl, HEAD_PAD), table_map),
                  pl.BlockSpec((tl, HEAD_PAD), table_map)],
        out_specs=pl.BlockSpec((tl, hw), lambda i: (i, 0)),
        out_shape=jax.ShapeDtypeStruct((nt, hw), MXU_DTYPE),
        compiler_params=_cp(("parallel",)),
        name="q_proj",
    )(proj, norm_g, wq, wq2, qg, qg2, cos, sin)


def _rope_tables(tl, dec_seq):
    a = ROPE_DIM // 2
    t = np.arange(dec_seq)
    pos = np.stack([t // GRID_W, t % GRID_W]).astype(np.float32)
    inv = np.power(np.float32(ROPE_BASE), -np.arange(0, a, 2, dtype=np.float32) / np.float32(a))
    ang = pos[:, :, None] * inv.astype(np.float32)
    cos = np.ones((tl + dec_seq, HEAD_PAD), np.float32)
    sin = np.zeros((tl + dec_seq, HEAD_PAD), np.float32)
    hq = a // 2
    for i in range(2):
        lo = NOPE_DIM + i * a
        cos[tl:, lo:lo + hq] = np.cos(ang[i])
        cos[tl:, lo + hq:lo + a] = np.cos(ang[i])
        sin[tl:, lo:lo + hq] = -np.sin(ang[i])
        sin[tl:, lo + hq:lo + a] = np.sin(ang[i])
    return jnp.asarray(cos), jnp.asarray(sin)


def _attn_kernel(*refs, has_ctx, chunk):
    if has_ctx:
        q_ref, k_ref, v_ref, kc_ref, vc_ref, o_ref = refs
    else:
        q_ref, k_ref, v_ref, o_ref = refs
    nt_dims = (((1,), (1,)), ((), ()))
    tq = q_ref.shape[0]
    length = k_ref.shape[0]
    chunks = [(k_ref, v_ref, j, min(chunk, length - j)) for j in range(0, length, chunk)]
    if has_ctx:
        chunks = [(kc_ref, vc_ref, 0, kc_ref.shape[0])] + chunks
    heads = [slice(h * HEAD_PAD, (h + 1) * HEAD_PAD) for h in range(2)]
    qs = [q_ref[:, hs] for hs in heads]
    m = [jnp.full((tq, 1), -jnp.inf, F32) for _ in heads]
    acc = [jnp.zeros((tq, HEAD_PAD), F32) for _ in heads]
    for kr, vr, start, size in chunks:
        rows = pl.ds(start, size)
        for h, hs in enumerate(heads):
            s = lax.dot_general(qs[h], kr[rows, hs], nt_dims, preferred_element_type=F32)
            m_new = jnp.maximum(m[h], jnp.max(s, axis=-1, keepdims=True))
            p = jnp.exp2(s - m_new).astype(MXU_DTYPE)
            acc[h] = acc[h] * jnp.exp2(m[h] - m_new) + jnp.dot(p, vr[rows, hs],
                                                               preferred_element_type=F32)
            m[h] = m_new
    outs = [a[:, 0:V_DIM] / a[:, V_DIM:V_DIM + 1] for a in acc]
    o_ref[...] = jnp.concatenate(outs, axis=-1).astype(o_ref.dtype)


def _attention(q, k, v, out_rows, row_off, bsz, length, tq, ctx=None):
    assert row_off % length == 0 and length % tq == 0
    qb0 = row_off // tq
    kb0 = row_off // length
    nq = length // tq
    hp = N_HEADS // 2
    in_specs = [pl.BlockSpec((tq, 2 * HEAD_PAD), lambda b, h, i: (qb0 + b * nq + i, h)),
                pl.BlockSpec((length, 2 * HEAD_PAD), lambda b, h, i: (kb0 + b, h)),
                pl.BlockSpec((length, 2 * HEAD_PAD), lambda b, h, i: (kb0 + b, h))]
    args = [q, k, v]
    if ctx is not None:
        kc, vc, past = ctx
        in_specs += [pl.BlockSpec((past, 2 * HEAD_PAD), lambda b, h, i: (b, h)),
                     pl.BlockSpec((past, 2 * HEAD_PAD), lambda b, h, i: (b, h))]
        args += [kc, vc]
    return pl.pallas_call(
        functools.partial(_attn_kernel, has_ctx=ctx is not None, chunk=min(length, 2048)),
        grid=(bsz, hp, nq),
        in_specs=in_specs,
        out_specs=pl.BlockSpec((tq, 2 * V_DIM), lambda b, h, i: (b * nq + i, h)),
        out_shape=jax.ShapeDtypeStruct((out_rows, N_HEADS * V_DIM), MXU_DTYPE),
        compiler_params=_cp(("parallel", "parallel", "arbitrary")),
        name="attention_ctx" if ctx is None else "attention_latent",
    )(*args)


def _merge_kernel(p_ref, h_ref, a_ref, g0_ref, g1_ref, g2_ref, x_ref, mod_ref,
                  wp_ref, wh_ref, wa_ref, wo_ref, o_ref):
    yp = jnp.dot(p_ref[...], wp_ref[...], preferred_element_type=F32)
    yh = jnp.dot(h_ref[...], wh_ref[...], preferred_element_type=F32)
    ya = jnp.dot(a_ref[...], wa_ref[...], preferred_element_type=F32)
    gate = lambda ref: _sigmoid(ref[...].astype(F32))
    merged = gate(g0_ref) * yp + gate(g1_ref) * yh + gate(g2_ref) * ya
    mix = jnp.dot(merged.astype(MXU_DTYPE), wo_ref[...], preferred_element_type=F32)
    o_ref[...] = x_ref[...] + mod_ref[0][2:3] * mix


def _merge(p, hy, att, proj, x, mod, wp, wh, wa, wo, tm, ctx_rows, dec_seq):
    nt, d = x.shape
    row = lambda i: (i, 0)
    const = lambda i: (0, 0)
    g0 = OFF_G // d
    return pl.pallas_call(
        _merge_kernel,
        grid=(nt // tm,),
        in_specs=[pl.BlockSpec((tm, POOL_WIDTH), row),
                  pl.BlockSpec((tm, HY_WIDTH), row),
                  pl.BlockSpec((tm, N_HEADS * V_DIM), row),
                  pl.BlockSpec((tm, d), lambda i: (i, g0)),
                  pl.BlockSpec((tm, d), lambda i: (i, g0 + 1)),
                  pl.BlockSpec((tm, d), lambda i: (i, g0 + 2)),
                  pl.BlockSpec((tm, d), row),
                  pl.BlockSpec((1, 6, d), _mod_row_map(tm, ctx_rows, dec_seq)),
                  pl.BlockSpec((POOL_WIDTH, d), const),
                  pl.BlockSpec((HY_WIDTH, d), const),
                  pl.BlockSpec((N_HEADS * V_DIM, d), const),
                  pl.BlockSpec((d, d), const)],
        out_specs=pl.BlockSpec((tm, d), row),
        out_shape=jax.ShapeDtypeStruct((nt, d), F32),
        compiler_params=_cp(("parallel",)),
        name="merge",
    )(p, hy, att, proj, proj, proj, x, mod, wp, wh, wa, wo)


def _norm2(x_ref, mod_ref, g_ref):
    m = mod_ref[0]
    return (_rms(x_ref[...], D_MODEL) * g_ref[...]) * (1.0 + m[4:5]) + m[3:4]


def _ffn_kernel(x_ref, mod_ref, g_ref, w1_ref, w3_ref, w2_ref, o_ref, *, sub):
    m = mod_ref[0]
    g = g_ref[...]
    for j in range(x_ref.shape[0] // sub):
        rs = pl.ds(j * sub, sub)
        x = x_ref[rs, :]
        h = ((_rms(x, D_MODEL) * g) * (1.0 + m[4:5]) + m[3:4]).astype(MXU_DTYPE)
        a = jnp.dot(h, w1_ref[...], preferred_element_type=F32)
        b = jnp.dot(h, w3_ref[...], preferred_element_type=F32)
        f = jnp.dot((_silu(a) * b).astype(MXU_DTYPE), w2_ref[...], preferred_element_type=F32)
        o_ref[rs, :] = x + m[5:6] * f


def _ffn_dense(x, mod, norm_g, w1, w3, w2, tm, ctx_rows, dec_seq):
    nt, d = x.shape
    ff = w1.shape[1]
    resident = lambda shape: pl.BlockSpec(shape, lambda i: (0, 0), pipeline_mode=pl.Buffered(1))
    return pl.pallas_call(
        functools.partial(_ffn_kernel, sub=min(tm, 256)),
        grid=(nt // tm,),
        in_specs=[pl.BlockSpec((tm, d), lambda i: (i, 0)),
                  pl.BlockSpec((1, 6, d), _mod_row_map(tm, ctx_rows, dec_seq)),
                  pl.BlockSpec((1, d), lambda i: (0, 0)),
                  resident((d, ff)), resident((d, ff)), resident((ff, d))],
        out_specs=pl.BlockSpec((tm, d), lambda i: (i, 0)),
        out_shape=jax.ShapeDtypeStruct((nt, d), F32),
        compiler_params=_cp(("parallel",)),
        name="ffn_dense",
    )(x, mod, norm_g, w1, w3, w2)


def _router_kernel(x_ref, mod_ref, g_ref, r_ref, h_ref, sel_ref):
    h = _norm2(x_ref, mod_ref, g_ref)
    h_ref[...] = h
    logits = _mm_f32(h, r_ref[...])
    lane = lax.broadcasted_iota(jnp.int32, logits.shape, 1).astype(F32)
    neg = jnp.float32(-jnp.inf)
    logits = jnp.where(lane < N_EXPERTS, logits, neg)
    m1 = jnp.max(logits, axis=-1, keepdims=True)
    i1 = jnp.min(jnp.where(logits == m1, lane, float(LANE)), axis=-1, keepdims=True)
    rest = jnp.where(lane == i1, neg, logits)
    m2 = jnp.max(rest, axis=-1, keepdims=True)
    i2 = jnp.min(jnp.where(rest == m2, lane, float(LANE)), axis=-1, keepdims=True)
    e2 = jnp.exp(m2 - m1)
    w_top = 1.0 / (1.0 + e2)
    sel_ref[...] = (jnp.where(lane == 0.0, i1, 0.0) + jnp.where(lane == 1.0, i2, 0.0)
                    + jnp.where(lane == 2.0, w_top, 0.0) + jnp.where(lane == 3.0, e2 * w_top, 0.0))


def _router(x, mod, norm_g, router, tm, ctx_rows, dec_seq):
    nt, d = x.shape
    rp = jnp.pad(router, ((0, 0), (0, LANE - N_EXPERTS)))
    return pl.pallas_call(
        _router_kernel,
        grid=(nt // tm,),
        in_specs=[pl.BlockSpec((tm, d), lambda i: (i, 0)),
                  pl.BlockSpec((1, 6, d), _mod_row_map(tm, ctx_rows, dec_seq)),
                  pl.BlockSpec((1, d), lambda i: (0, 0)),
                  pl.BlockSpec((d, LANE), lambda i: (0, 0))],
        out_specs=[pl.BlockSpec((tm, d), lambda i: (i, 0)),
                   pl.BlockSpec((tm, LANE), lambda i: (i, 0))],
        out_shape=[jax.ShapeDtypeStruct((nt, d), F32),
                   jax.ShapeDtypeStruct((nt, LANE), F32)],
        compiler_params=_cp(("parallel",)),
        name="router",
    )(x, mod, norm_g, rp)


def _dispatch_plan(sel, rows_per_tile):
    nt = sel.shape[0]
    n_pairs = 2 * nt
    r = rows_per_tile
    n_tiles = n_pairs // r + N_EXPERTS
    e_p = sel[:, 0:2].astype(jnp.int32).T.reshape(n_pairs)
    experts = jnp.arange(N_EXPERTS, dtype=jnp.int32)
    onehot = (e_p[:, None] == experts[None, :]).astype(jnp.int32)
    csum = jnp.cumsum(onehot, axis=0)
    rank = jnp.sum((csum - onehot) * onehot, axis=1)
    counts = csum[-1]
    padded = (counts + r - 1) // r * r
    ends = jnp.cumsum(padded)
    starts = ends - padded
    slot_p = jnp.sum(onehot * starts[None, :], axis=1) + rank
    tile_start = jnp.arange(n_tiles, dtype=jnp.int32) * r
    tile_expert = jnp.minimum(jnp.sum((tile_start[:, None] >= ends[None, :]).astype(jnp.int32), axis=1),
                              N_EXPERTS - 1)
    run_end = jnp.sum((tile_expert[:, None] == experts[None, :]) * (starts + counts)[None, :], axis=1)
    tile_rows = jnp.clip(run_end - tile_start, 0, r)
    return tile_expert, tile_rows, slot_p


def _row_copies(src, src_rows, dst, dst_rows, sem, n):
    return [pltpu.make_async_copy(src.at[pl.ds(src_rows(i), 1), :], dst.at[pl.ds(dst_rows(i), 1), :], sem)
            for i in range(n)]


def _start_and_wait(copies):
    for i, cp in enumerate(copies):
        cp.start(priority=i % 2)
    for cp in copies:
        cp.wait()


def _dispatch_kernel(idx_ref, h_ref, xs_hbm, sem, *, n):
    copies = _row_copies(h_ref, lambda i: i, xs_hbm, lambda i: idx_ref[0, 0, i], sem, n)
    _start_and_wait(copies)


def _moe_dispatch(h, slot_p, n_slots, pb):
    nt, d = h.shape
    nb = nt // pb
    n_blocks = slot_p.shape[0] // pb
    return pl.pallas_call(
        functools.partial(_dispatch_kernel, n=pb),
        grid=(n_blocks,),
        in_specs=[pl.BlockSpec((1, 1, pb), lambda i: (i, 0, 0), memory_space=pltpu.SMEM),
                  pl.BlockSpec((pb, d), lambda i: (i % nb, 0))],
        out_specs=pl.BlockSpec(memory_space=pl.ANY),
        out_shape=jax.ShapeDtypeStruct((n_slots, d), F32),
        scratch_shapes=[pltpu.SemaphoreType.DMA(())],
        compiler_params=_cp(("arbitrary",)),
        name="moe_dispatch",
    )(slot_p.reshape(n_blocks, 1, pb), h)


def _moe_kernel(te_ref, tr_ref, xs_ref, w13_ref, w2_ref, ys_ref, *, rows, sub):
    n_rows = tr_ref[pl.program_id(0)]
    ff = w2_ref.shape[1]

    @pl.when(n_rows > 0)
    def _():
        for j in range(rows // sub):
            rs = pl.ds(j * sub, sub)
            occupied = j * sub + lax.broadcasted_iota(jnp.int32, (sub, 1), 0) < n_rows
            xb = jnp.where(occupied, xs_ref[rs, :], 0.0).astype(MXU_DTYPE)
            ab = jnp.dot(xb, w13_ref[0], preferred_element_type=F32)
            gated = (_silu(ab[:, 0:ff]) * ab[:, ff:2 * ff]).astype(MXU_DTYPE)
            ys_ref[rs, :] = jnp.dot(gated, w2_ref[0], preferred_element_type=F32)

    @pl.when(n_rows == 0)
    def _():
        ys_ref[...] = jnp.zeros_like(ys_ref)


def _ffn_moe(xs, tile_expert, tile_rows, w13, w2, rows):
    n_slots, d = xs.shape
    _, ff, _ = w2.shape
    wspec = lambda shape: pl.BlockSpec((1,) + shape, lambda t, te, tr: (te[t], 0, 0))
    grid_spec = pltpu.PrefetchScalarGridSpec(
        num_scalar_prefetch=2,
        grid=(n_slots // rows,),
        in_specs=[pl.BlockSpec((rows, d), lambda t, te, tr: (t, 0)),
                  wspec((d, 2 * ff)), wspec((ff, d))],
        out_specs=pl.BlockSpec((rows, d), lambda t, te, tr: (t, 0)))
    return pl.pallas_call(
        functools.partial(_moe_kernel, rows=rows, sub=min(rows, 256)),
        grid_spec=grid_spec,
        out_shape=jax.ShapeDtypeStruct((n_slots, d), F32),
        compiler_params=_cp(("arbitrary",)),
        name="ffn_moe",
    )(tile_expert, tile_rows, xs, w13, w2)


def _combine_kernel(idx0_ref, idx1_ref, x_ref, mod_ref, sel_ref, ys_hbm, o_ref, buf, sem, *, n):
    copies = (_row_copies(ys_hbm, lambda i: idx0_ref[0, 0, i], buf.at[0], lambda i: i, sem, n)
              + _row_copies(ys_hbm, lambda i: idx1_ref[0, 0, i], buf.at[1], lambda i: i, sem, n))
    _start_and_wait(copies)
    sel = sel_ref[...]
    f = sel[:, 2:3] * buf[0] + sel[:, 3:4] * buf[1]
    o_ref[...] = x_ref[...] + mod_ref[0][5:6] * f


def _moe_combine(x, mod, sel, ys, slot_p, tm, ctx_rows, dec_seq):
    nt, d = x.shape
    nb = nt // tm
    idx = slot_p.reshape(2 * nb, 1, tm)
    row = lambda i: (i, 0)
    return pl.pallas_call(
        functools.partial(_combine_kernel, n=tm),
        grid=(nb,),
        in_specs=[pl.BlockSpec((1, 1, tm), lambda i: (i, 0, 0), memory_space=pltpu.SMEM),
                  pl.BlockSpec((1, 1, tm), lambda i: (nb + i, 0, 0), memory_space=pltpu.SMEM),
                  pl.BlockSpec((tm, d), row),
                  pl.BlockSpec((1, 6, d), _mod_row_map(tm, ctx_rows, dec_seq)),
                  pl.BlockSpec((tm, LANE), row),
                  pl.BlockSpec(memory_space=pl.ANY)],
        out_specs=pl.BlockSpec((tm, d), row),
        out_shape=jax.ShapeDtypeStruct((nt, d), F32),
        scratch_shapes=[pltpu.VMEM((2, tm, d), F32), pltpu.SemaphoreType.DMA(())],
        compiler_params=_cp(("arbitrary",)),
        name="moe_combine",
    )(idx, idx, x, mod, sel, ys)


def _prep_w_in(w_in):
    depth, d, _ = w_in.shape
    c1 = POOL_WIDTH
    c2 = c1 + 3 * HY_WIDTH
    c3 = c2 + Q_RANK
    c4 = c3 + KV_RANK
    c5 = c4 + ROPE_DIM
    z = lambda n: jnp.zeros((depth, d, n), w_in.dtype)
    kr_group = jnp.concatenate([z(NOPE_DIM), w_in[:, :, c4:c5], z(HEAD_PAD - QK_DIM)], axis=-1)
    cols = [w_in[:, :, c1:c2], w_in[:, :, 0:c1], w_in[:, :, c3:c4],
            kr_group, _partner_lanes(kr_group),
            w_in[:, :, c2:c3], z(Q_PAD - Q_RANK), w_in[:, :, c5:]]
    out = jnp.concatenate(cols, axis=-1).astype(MXU_DTYPE)
    assert out.shape[-1] == IN_PAD
    return out


def _head_pad_cols(w, width):
    lead = w.shape[:-1]
    w = w.reshape(*lead, N_HEADS, width)
    w = jnp.pad(w, [(0, 0)] * len(lead) + [(0, 0), (0, HEAD_PAD - width)])
    return w.reshape(*lead, N_HEADS * HEAD_PAD)


def _pad_last(a, n):
    return jnp.pad(a, [(0, 0)] * (a.ndim - 1) + [(0, n - a.shape[-1])])


def kernel(x_prompt, x_sample, cache_ckv, cache_krope, c, c_ctx, w_mod, b_mod, norm1_g, norm2_g, w_in, pool_w, pool_scale, pool_out, hy_conv_w, hy_conv_b, hy_w1, hy_b1, hy_w2, hy_b2, hy_w3, hy_freq, hy_decay, hy_bias, hy_out, q_norm_g, w_qb, kv_norm_g, w_kvb, qk_q_g, qk_k_g, mla_out, w_out, ffn_w1, ffn_w3, ffn_w2, moe_router, moe_w1, moe_w3, moe_w2):
    batch, seq, d = x_prompt.shape
    dec_batch, dec_seq, _ = x_sample.shape
    depth = w_mod.shape[0]
    past = cache_ckv.shape[2]
    ctx_rows = batch * seq
    lat_rows = dec_batch * dec_seq
    nt = ctx_rows + lat_rows
    tl = 256
    assert seq % tl == 0 and dec_seq % tl == 0 and ctx_rows % dec_seq == 0 and past % tl == 0
    tm = min(1024, math.gcd(ctx_rows, dec_seq))
    n_ctx_tiles = ctx_rows // tl
    bf = lambda a: a.astype(MXU_DTYPE)

    w_in_p = _prep_w_in(w_in)
    eye = jnp.eye(POOL_GROUPS, dtype=pool_w.dtype)
    gc = POOL_WIDTH // POOL_GROUPS
    pool_bd = bf(jnp.einsum('lgcd,gh->lgchd', pool_w, eye).reshape(depth, POOL_WIDTH, POOL_WIDTH))
    kvb = w_kvb.reshape(depth, KV_RANK, N_HEADS, NOPE_DIM + V_DIM)
    wk = bf(_head_pad_cols(kvb[..., :NOPE_DIM].reshape(depth, KV_RANK, N_HEADS * NOPE_DIM), NOPE_DIM))
    wv = bf(_head_pad_cols(kvb[..., NOPE_DIM:].reshape(depth, KV_RANK, N_HEADS * V_DIM), V_DIM))
    vone = jnp.tile((jnp.arange(HEAD_PAD) == V_DIM).astype(F32), N_HEADS)[None, :]
    wq_f = jnp.pad(_head_pad_cols(w_qb, QK_DIM), ((0, 0), (0, Q_PAD - Q_RANK), (0, 0)))
    wq, wq2 = bf(wq_f), bf(_partner_lanes(wq_f))
    q_norm_p = _pad_last(q_norm_g, Q_PAD)
    qg = _pad_last(qk_q_g, HEAD_PAD)
    kg = _pad_last(qk_k_g, HEAD_PAD)
    qg2, kg2 = _partner_lanes(qg), _partner_lanes(kg)
    pool_out_b, hy_out_b, mla_out_b, w_out_b = bf(pool_out), bf(hy_out), bf(mla_out), bf(w_out)
    ffn_w1_b, ffn_w3_b, ffn_w2_b = bf(ffn_w1), bf(ffn_w3), bf(ffn_w2)
    moe_w13_b, moe_w2_b = bf(jnp.concatenate([moe_w1, moe_w3], axis=-1)), bf(moe_w2)

    tp = math.gcd(512, math.gcd(ctx_rows, math.gcd(dec_seq, dec_batch * past)))
    cos_t, sin_t = _rope_tables(tp, dec_seq)
    n_ctx_tp = ctx_rows // tp
    n_lat_tp = dec_seq // tp
    table_map = lambda i: (jnp.where(i < n_ctx_tp, 0, 1 + (i - n_ctx_tp) % n_lat_tp), 0)
    ident_map = lambda i: (0, 0)
    mats_ctx = _dft_matrices(seq)
    mats_lat = _dft_matrices(dec_seq)

    n_cond = 1 + dec_batch
    cond_rows = -(-n_cond // SUBLANE) * SUBLANE
    cond = jnp.concatenate([c_ctx[None, :], c, jnp.zeros((cond_rows - n_cond, d), c.dtype)], axis=0)
    mod_all = _modulation(cond, w_mod, b_mod)[:, :n_cond].reshape(depth, n_cond, 6, d)
    filt_ctx = _hyena_filters(seq, hy_w1, hy_b1, hy_w2, hy_b2, hy_w3, hy_freq, hy_decay)
    filt_lat = _hyena_filters(dec_seq, hy_w1, hy_b1, hy_w2, hy_b2, hy_w3, hy_freq, hy_decay)
    cache_kr_p = jnp.pad(cache_krope, ((0, 0), (0, 0), (0, 0), (NOPE_DIM, HEAD_PAD - QK_DIM)))
    cache_kr_p = jnp.concatenate([cache_kr_p, _partner_lanes(cache_kr_p)], axis=-1)

    x = jnp.concatenate([x_prompt.reshape(ctx_rows, d), x_sample.reshape(lat_rows, d)], axis=0)
    ckv_list, kr_list = [], []
    for i in range(depth):
        j = i // 2
        mod = mod_all[i]
        r1 = lambda a: a[i].reshape(1, -1)
        proj = _inproj(x, mod, r1(norm1_g), w_in_p[i], tm, ctx_rows, dec_seq)

        vx, vxb, x0, pooled = _local_mixers(proj, hy_conv_w[i], r1(hy_conv_b), pool_bd[i], r1(pool_scale),
                                            tl, n_ctx_tiles, seq, dec_seq)
        bias = r1(hy_bias)
        hy_c = _long_conv(vx[:ctx_rows], vxb[:ctx_rows], x0[:ctx_rows], filt_ctx[i], bias,
                          mats_ctx, batch, seq)
        hy_l = _long_conv(vx[ctx_rows:], vxb[ctx_rows:], x0[ctx_rows:], filt_lat[i], bias,
                          mats_lat, dec_batch, dec_seq)
        hy = jnp.concatenate([hy_c, hy_l], axis=0)

        ckv, k, v = _kv_proj(proj, OFF_KV // KV_RANK, proj, OFF_KROPE // HEAD_PAD, OFF_KROPE2 // HEAD_PAD,
                             r1(kv_norm_g), wk[i], wv[i], vone, r1(kg), r1(kg2), cos_t, sin_t,
                             table_map, tp, True)
        kc, vc = _kv_proj(cache_ckv[:, i].reshape(dec_batch * past, KV_RANK), 0,
                          cache_kr_p[:, i].reshape(dec_batch * past, 2 * HEAD_PAD), 0, 1,
                          r1(kv_norm_g), wk[i], wv[i], vone, r1(kg), r1(kg2), cos_t, sin_t,
                          ident_map, tp, False)
        q = _q_proj(proj, r1(q_norm_p), wq[i], wq2[i], r1(qg), r1(qg2), cos_t, sin_t, table_map, tp)
        att_c = _attention(q, k, v, ctx_rows, 0, batch, seq, min(seq, 256))
        att_l = _attention(q, k, v, lat_rows, ctx_rows, dec_batch, dec_seq, min(dec_seq, 1024),
                           ctx=(kc, vc, past))
        att = jnp.concatenate([att_c, att_l], axis=0)

        x = _merge(pooled, hy, att, proj, x, mod, pool_out_b[i], hy_out_b[i], mla_out_b[i], w_out_b[i],
                   min(tm, 512), ctx_rows, dec_seq)

        if i % 2 == 1:
            h2, sel = _router(x, mod, r1(norm2_g), moe_router[j], tm, ctx_rows, dec_seq)
            tile_expert, tile_rows, slot_p = _dispatch_plan(sel, MOE_ROWS)
            xs = _moe_dispatch(h2, slot_p, tile_expert.shape[0] * MOE_ROWS, min(tm, 512))
            ys = _ffn_moe(xs, tile_expert, tile_rows, moe_w13_b[j], moe_w2_b[j], MOE_ROWS)
            x = _moe_combine(x, mod, sel, ys, slot_p, min(tm, 512), ctx_rows, dec_seq)
        else:
            x = _ffn_dense(x, mod, r1(norm2_g), ffn_w1_b[j], ffn_w3_b[j], ffn_w2_b[j],
                           tm, ctx_rows, dec_seq)

        ckv_list.append(ckv[:ctx_rows].reshape(batch, seq, KV_RANK))
        kr_list.append(proj[:ctx_rows, OFF_KROPE + NOPE_DIM:OFF_KROPE + QK_DIM].astype(F32)
                       .reshape(batch, seq, ROPE_DIM))

    y_p = x[:ctx_rows].reshape(batch, seq, d)
    y_s = x[ctx_rows:].reshape(dec_batch, dec_seq, d)
    return (y_p, y_s, jnp.stack(ckv_list, axis=1), jnp.stack(kr_list, axis=1))
```

```python
import functools
import math

import numpy as np
import jax
import jax.numpy as jnp
from jax import lax
from jax.experimental import pallas as pl
from jax.experimental.pallas import tpu as pltpu

F32 = jnp.float32
MXU_DTYPE = jnp.bfloat16
HIGHEST = lax.Precision.HIGHEST

D_MODEL = 1024
GRID_W = 64
EPS = 1e-6
POOL_WIDTH = 256
POOL_GROUPS = 4
POOL_WINDOWS = (2, 4, 8, 16)
HY_WIDTH = 256
HY_SHORT = 3
HY_BANDS = 8
HY_EMB = 1 + 2 * HY_BANDS
HY_FFN = 64
N_HEADS = 8
Q_RANK = 384
KV_RANK = 256
NOPE_DIM = 64
ROPE_DIM = 32
V_DIM = 64
QK_DIM = NOPE_DIM + ROPE_DIM
ROPE_BASE = 10000.0
N_EXPERTS = 8

LANE = 128
SUBLANE = 8
HALO = 16
HEAD_PAD = LANE
Q_PAD = 512
MOE_ROWS = 512
VMEM_LIMIT = 56 * 1024 * 1024
SOFTMAX_C = (QK_DIM ** -0.5) * math.log2(math.e)

OFF_HY = 0
OFF_POOL = 3 * HY_WIDTH
OFF_KV = OFF_POOL + POOL_WIDTH
OFF_KROPE = OFF_KV + KV_RANK
OFF_KROPE2 = OFF_KROPE + HEAD_PAD
OFF_Q = 1536
OFF_G = OFF_Q + Q_PAD
IN_PAD = OFF_G + 3 * D_MODEL


def _cp(sem, vmem=VMEM_LIMIT):
    return pltpu.CompilerParams(dimension_semantics=sem, vmem_limit_bytes=vmem)


def _mm(a, b):
    return jnp.dot(a.astype(MXU_DTYPE), b.astype(MXU_DTYPE), preferred_element_type=F32)


def _mm_f32(a, b):
    return jnp.dot(a, b, preferred_element_type=F32, precision=HIGHEST)


def _sigmoid(x):
    return 1.0 / (1.0 + jnp.exp(-x))


def _silu(x):
    return x * _sigmoid(x)


def _rms(x, n):
    ms = jnp.sum(x * x, axis=-1, keepdims=True) * (1.0 / n)
    return x * lax.rsqrt(ms + EPS)


def _mod_row_map(tm, ctx_rows, dec_seq):
    def index_map(i, *_):
        row0 = i * tm
        return (jnp.where(row0 < ctx_rows, 0, 1 + (row0 - ctx_rows) // dec_seq), 0, 0)
    return index_map


def _mod_kernel(c_ref, w_ref, b_ref, o_ref):
    o_ref[0] = _mm_f32(_silu(c_ref[...]), w_ref[0]) + b_ref[0]


def _modulation(cond, w_mod, b_mod):
    depth, d, n6 = w_mod.shape
    r = cond.shape[0]
    tn = 1024
    return pl.pallas_call(
        _mod_kernel,
        grid=(depth, n6 // tn),
        in_specs=[pl.BlockSpec((r, d), lambda l, j: (0, 0)),
                  pl.BlockSpec((1, d, tn), lambda l, j: (l, 0, j)),
                  pl.BlockSpec((1, 1, tn), lambda l, j: (l, 0, j))],
        out_specs=pl.BlockSpec((1, r, tn), lambda l, j: (l, 0, j)),
        out_shape=jax.ShapeDtypeStruct((depth, r, n6), F32),
        compiler_params=_cp(("parallel", "parallel")),
        name="modulation",
    )(cond, w_mod, b_mod.reshape(depth, 1, n6))


def _inproj_kernel(x_ref, mod_ref, g_ref, w_ref, o_ref, *, sub):
    m = mod_ref[0]
    g = g_ref[...]
    for j in range(x_ref.shape[0] // sub):
        rs = pl.ds(j * sub, sub)
        h = ((_rms(x_ref[rs, :], D_MODEL) * g) * (1.0 + m[1:2]) + m[0:1]).astype(MXU_DTYPE)
        o_ref[rs, :] = jnp.dot(h, w_ref[...], preferred_element_type=F32).astype(o_ref.dtype)


def _inproj(x, mod, norm_g, w, tm, ctx_rows, dec_seq):
    nt, d = x.shape
    n = w.shape[1]
    return pl.pallas_call(
        functools.partial(_inproj_kernel, sub=min(tm, 256)),
        grid=(nt // tm,),
        in_specs=[pl.BlockSpec((tm, d), lambda i: (i, 0)),
                  pl.BlockSpec((1, 6, d), _mod_row_map(tm, ctx_rows, dec_seq)),
                  pl.BlockSpec((1, d), lambda i: (0, 0)),
                  pl.BlockSpec((d, n), lambda i: (0, 0), pipeline_mode=pl.Buffered(1))],
        out_specs=pl.BlockSpec((tm, n), lambda i: (i, 0)),
        out_shape=jax.ShapeDtypeStruct((nt, n), MXU_DTYPE),
        compiler_params=_cp(("parallel",)),
        name="inproj",
    )(x, mod, norm_g, w)


def _local_kernel(u_ref, prev_ref, next_ref, cw_ref, cb_ref, pw_ref, ps_ref,
                  vx_ref, vxb_ref, x0_ref, p_ref, ext_ref, *, tl, tile0, n_ctx_tiles, seq, dec_seq):
    i = pl.program_id(0) + tile0
    row0 = i * tl
    ctx = i < n_ctx_tiles
    p0 = jnp.where(ctx, row0 % seq, (row0 - n_ctx_tiles * tl) % dec_seq)
    length = jnp.where(ctx, seq, dec_seq)
    first = p0 == 0
    last = p0 + tl == length

    ext_ref[0:SUBLANE, :] = jnp.where(first, 0.0, prev_ref[HALO - SUBLANE:HALO, :].astype(F32))
    ext_ref[SUBLANE:SUBLANE + tl, :] = u_ref[...].astype(F32)
    ext_ref[SUBLANE + tl:, :] = jnp.where(last, 0.0, next_ref[0:SUBLANE, :].astype(F32))

    hw = 3 * HY_WIDTH
    cw = cw_ref[...]
    z = (cb_ref[...]
         + ext_ref[pl.ds(SUBLANE - 1, tl), 0:hw] * cw[0:1]
         + ext_ref[pl.ds(SUBLANE, tl), 0:hw] * cw[1:2]
         + ext_ref[pl.ds(SUBLANE + 1, tl), 0:hw] * cw[2:3])
    x0 = z[:, 0:HY_WIDTH]
    vx = z[:, 2 * HY_WIDTH:3 * HY_WIDTH] * z[:, HY_WIDTH:2 * HY_WIDTH]
    vx_ref[...] = vx
    vxb_ref[...] = vx.astype(vxb_ref.dtype)
    x0_ref[...] = x0

    def e(j):
        return ext_ref[pl.ds(SUBLANE + j, tl), hw:hw + POOL_WIDTH]

    u = e(0)
    s2 = e(-1) + u
    s4 = s2 + e(-2) + e(1)
    s8 = s4 + e(-4) + e(-3) + e(2) + e(3)
    s16 = s8 + e(-8) + e(-7) + e(-6) + e(-5) + e(4) + e(5) + e(6) + e(7)
    lane = lax.broadcasted_iota(jnp.int32, (tl, POOL_WIDTH), 1)
    gc = POOL_WIDTH // POOL_GROUPS
    ssum = jnp.where(lane < gc, s2, jnp.where(lane < 2 * gc, s4, jnp.where(lane < 3 * gc, s8, s16)))
    half = jnp.where(lane < gc, 1, jnp.where(lane < 2 * gc, 2, jnp.where(lane < 3 * gc, 4, 8)))
    t = p0 + lax.broadcasted_iota(jnp.int32, (tl, POOL_WIDTH), 0)
    cnt = jnp.minimum(t + half, length) - jnp.maximum(t - half, 0)
    pooled = ssum / cnt.astype(F32) - u
    p_ref[...] = (_mm(pooled, pw_ref[...]) * ps_ref[...]).astype(p_ref.dtype)


def _local_mixers(proj, conv_w, conv_b, pool_bd, pool_scale, tl, tile0, n_tiles, n_ctx_tiles, seq,
                  dec_seq):
    nt = n_tiles * tl
    width = 3 * HY_WIDTH + POOL_WIDTH
    nb = tl // HALO
    last_blk = proj.shape[0] // HALO - 1
    kern = functools.partial(_local_kernel, tl=tl, tile0=tile0, n_ctx_tiles=n_ctx_tiles, seq=seq,
                             dec_seq=dec_seq)
    row = lambda i: (i, 0)
    const = lambda i: (0, 0)
    return pl.pallas_call(
        kern,
        grid=(n_tiles,),
        in_specs=[pl.BlockSpec((tl, width), lambda i: (i + tile0, 0)),
                  pl.BlockSpec((HALO, width), lambda i: (jnp.maximum((i + tile0) * nb - 1, 0), 0)),
                  pl.BlockSpec((HALO, width),
                               lambda i: (jnp.minimum((i + tile0 + 1) * nb, last_blk), 0)),
                  pl.BlockSpec((HY_SHORT, 3 * HY_WIDTH), const),
                  pl.BlockSpec((1, 3 * HY_WIDTH), const),
                  pl.BlockSpec((POOL_WIDTH, POOL_WIDTH), const),
                  pl.BlockSpec((1, POOL_WIDTH), const)],
        out_specs=[pl.BlockSpec((tl, HY_WIDTH), row)] * 3 + [pl.BlockSpec((tl, POOL_WIDTH), row)],
        out_shape=[jax.ShapeDtypeStruct((nt, HY_WIDTH), F32),
                   jax.ShapeDtypeStruct((nt, HY_WIDTH), MXU_DTYPE),
                   jax.ShapeDtypeStruct((nt, HY_WIDTH), F32),
                   jax.ShapeDtypeStruct((nt, POOL_WIDTH), MXU_DTYPE)],
        scratch_shapes=[pltpu.VMEM((tl + 2 * SUBLANE, width), F32)],
        compiler_params=_cp(("parallel",)),
        name="local_mixers",
    )(proj, proj, proj, conv_w, conv_b, pool_bd, pool_scale)


def _filter_kernel(z_ref, w1_ref, b1_ref, w2_ref, b2_ref, w3_ref, fr_ref, dec_ref, o_ref):
    z = z_ref[...]
    fr = fr_ref[0]
    hdn = jnp.sin(fr * (_mm_f32(z, w1_ref[0]) + b1_ref[0]))
    hdn = jnp.sin(fr * (_mm_f32(hdn, w2_ref[0]) + b2_ref[0]))
    filt = _mm_f32(hdn, w3_ref[0])
    win = jnp.exp(-z[:, 0:1] * jnp.abs(dec_ref[0]))
    o_ref[0] = filt * jnp.concatenate([win, win], axis=-1)


def _hyena_filters(length, w1, b1, w2, b2, w3, freq, decay):
    depth = w1.shape[0]
    t = np.linspace(0.0, 1.0, length, dtype=np.float32)[:, None]
    wpos = (np.float32(2.0 * math.pi / length) * np.arange(length, dtype=np.float32))[:, None]
    bands = np.linspace(1e-4, HY_BANDS - 1, HY_BANDS, dtype=np.float32)[None, :]
    emb_pad = 32
    z = np.zeros((length, emb_pad), np.float32)
    z[:, 0:1] = t
    z[:, 1:1 + HY_BANDS] = np.cos(bands * wpos)
    z[:, 1 + HY_BANDS:HY_EMB] = -np.sin(bands * wpos)
    w1p = jnp.pad(w1, ((0, 0), (0, emb_pad - HY_EMB), (0, 0)))
    tl = min(length, 512)
    lay = lambda l, i: (l, 0, 0)
    r3 = lambda a: a.reshape(depth, 1, a.shape[-1])
    return pl.pallas_call(
        _filter_kernel,
        grid=(depth, length // tl),
        in_specs=[pl.BlockSpec((tl, emb_pad), lambda l, i: (i, 0)),
                  pl.BlockSpec((1, emb_pad, HY_FFN), lay),
                  pl.BlockSpec((1, 1, HY_FFN), lay),
                  pl.BlockSpec((1, HY_FFN, HY_FFN), lay),
                  pl.BlockSpec((1, 1, HY_FFN), lay),
                  pl.BlockSpec((1, HY_FFN, 2 * HY_WIDTH), lay),
                  pl.BlockSpec((1, 1, HY_FFN), lay),
                  pl.BlockSpec((1, 1, HY_WIDTH), lay)],
        out_specs=pl.BlockSpec((1, tl, 2 * HY_WIDTH), lambda l, i: (l, i, 0)),
        out_shape=jax.ShapeDtypeStruct((depth, length, 2 * HY_WIDTH), F32),
        compiler_params=_cp(("parallel", "parallel")),
        name="hyena_filters",
    )(jnp.asarray(z), w1p, r3(b1), w2, r3(b2), w3, r3(freq), r3(decay))


def _dft_matrices(length):
    n = 2 * length
    f = jnp.arange(length, dtype=jnp.int32)[:, None]
    s = jnp.arange(length, dtype=jnp.int32)[None, :]
    blk = 64
    theta = np.float32(2.0 * math.pi / n)
    a_hi = ((f * (s[:, 0:length // blk] * blk)) % n).astype(F32) * theta
    a_lo = ((f * s[:, 0:blk]) % n).astype(F32) * theta
    c_hi, s_hi = jnp.cos(a_hi)[:, :, None], jnp.sin(a_hi)[:, :, None]
    c_lo, s_lo = jnp.cos(a_lo)[:, None, :], jnp.sin(a_lo)[:, None, :]
    cos = (c_hi * c_lo - s_hi * s_lo).reshape(length, length)
    sin = (s_hi * c_lo + c_hi * s_lo).reshape(length, length)
    alt = jnp.where(jnp.arange(length) % 2 == 0, 1.0, -1.0).astype(F32)
    fwd_im = jnp.where(f == 0, alt[None, :], -sin)
    fwd = jnp.concatenate([cos, fwd_im], axis=0)
    inv_re = jnp.where(s == 0, 1.0 / n, (2.0 / n) * cos)
    inv_im = jnp.where(s == 0, alt[:, None] / n, (-2.0 / n) * sin)
    inv = jnp.concatenate([inv_re, inv_im], axis=1)
    return fwd.astype(MXU_DTYPE), inv.astype(MXU_DTYPE)


def _dft_fwd_kernel(g_ref, x_ref, o_ref, *, bg):
    @pl.when(pl.program_id(2) == 0)
    def _():
        o_ref[...] = jnp.zeros_like(o_ref)
    g = g_ref[...]
    for b in range(bg):
        o_ref[b] += jnp.dot(g, x_ref[b], preferred_element_type=F32)


def _dft_fwd(gmat, x, bg):
    bsz, length, cols = x.shape
    tm = min(2 * length, 1024)
    tk = min(length, 512)
    return pl.pallas_call(
        functools.partial(_dft_fwd_kernel, bg=bg),
        grid=(bsz // bg, 2 * length // tm, length // tk),
        in_specs=[pl.BlockSpec((tm, tk), lambda g, m, k: (m, k)),
                  pl.BlockSpec((bg, tk, cols), lambda g, m, k: (g, k, 0))],
        out_specs=pl.BlockSpec((bg, tm, cols), lambda g, m, k: (g, m, 0)),
        out_shape=jax.ShapeDtypeStruct((bsz, 2 * length, cols), F32),
        compiler_params=_cp(("parallel", "parallel", "arbitrary")),
        name="dft_fwd",
    )(gmat, x)


def _dft_inv_kernel(gc_ref, gs_ref, xr_ref, xi_ref, ar_ref, ai_ref, hb0_ref, vx_ref, x0_ref,
                    bias_ref, o_ref, acc_ref, *, bg, tk):
    k = pl.program_id(2)

    @pl.when(k == 0)
    def _():
        acc_ref[...] = jnp.zeros_like(acc_ref)

    c = HY_WIDTH
    ar = ar_ref[...]
    ai = ai_ref[...]
    hb0 = hb0_ref[...]
    f0 = (k * tk + lax.broadcasted_iota(jnp.int32, (tk, c), 0)) == 0
    kr = ar[:, 0:c] + ar[:, c:2 * c] - hb0
    ki = jnp.where(f0, ai[:, 0:c] + ai[:, c:2 * c] - hb0, ai[:, 0:c] - ai[:, c:2 * c])
    gc = gc_ref[...]
    gs = gs_ref[...]
    for b in range(bg):
        xr = xr_ref[b]
        xi = xi_ref[b]
        zr = jnp.where(f0, xr * kr, xr * kr - xi * ki)
        zi = jnp.where(f0, xi * ki, xr * ki + xi * kr)
        acc_ref[b] += (jnp.dot(gc, zr.astype(MXU_DTYPE), preferred_element_type=F32)
                       + jnp.dot(gs, zi.astype(MXU_DTYPE), preferred_element_type=F32))

    @pl.when(k == pl.num_programs(2) - 1)
    def _():
        y = (acc_ref[...] + vx_ref[...] * bias_ref[...]) * x0_ref[...]
        o_ref[...] = y.astype(o_ref.dtype)


def _dft_inv(ginv, xspec, fspec, hb0, vx, x0, bias, bg):
    bsz, length, c = vx.shape
    tm = min(length, 512)
    tk = min(length, 512)
    nk = length // tk
    return pl.pallas_call(
        functools.partial(_dft_inv_kernel, bg=bg, tk=tk),
        grid=(bsz // bg, length // tm, nk),
        in_specs=[pl.BlockSpec((tm, tk), lambda g, m, k: (m, k)),
                  pl.BlockSpec((tm, tk), lambda g, m, k: (m, nk + k)),
                  pl.BlockSpec((bg, tk, c), lambda g, m, k: (g, k, 0)),
                  pl.BlockSpec((bg, tk, c), lambda g, m, k: (g, nk + k, 0)),
                  pl.BlockSpec((tk, 2 * c), lambda g, m, k: (k, 0)),
                  pl.BlockSpec((tk, 2 * c), lambda g, m, k: (nk + k, 0)),
                  pl.BlockSpec((1, c), lambda g, m, k: (0, 0)),
                  pl.BlockSpec((bg, tm, c), lambda g, m, k: (g, m, 0)),
                  pl.BlockSpec((bg, tm, c), lambda g, m, k: (g, m, 0)),
                  pl.BlockSpec((1, c), lambda g, m, k: (0, 0))],
        out_specs=pl.BlockSpec((bg, tm, c), lambda g, m, k: (g, m, 0)),
        out_shape=jax.ShapeDtypeStruct((bsz, length, c), MXU_DTYPE),
        scratch_shapes=[pltpu.VMEM((bg, tm, c), F32)],
        compiler_params=_cp(("parallel", "parallel", "arbitrary")),
        name="dft_inv",
    )(ginv, ginv, xspec, xspec, fspec, fspec, hb0, vx, x0, bias)


def _long_conv(vx, vxb, x0, filt, bias, mats, bsz, length):
    gfwd, ginv = mats
    c = vx.shape[-1]
    bg = math.gcd(bsz, 8)
    fspec = _dft_fwd(gfwd, filt.astype(MXU_DTYPE)[None], 1)[0]
    xspec = _dft_fwd(gfwd, vxb.reshape(bsz, length, c), bg)
    hb0 = filt[0:1, c:2 * c]
    out = _dft_inv(ginv, xspec, fspec, hb0, vx.reshape(bsz, length, c), x0.reshape(bsz, length, c),
                   bias, bg)
    return out.reshape(bsz * length, c)


def _rope_partner(r):
    a = ROPE_DIM // 2
    return a * (r // a) + (r % a + a // 2) % a


def _partner_lanes(a):
    idx = np.arange(a.shape[-1])
    r = idx % HEAD_PAD - NOPE_DIM
    rot = (r >= 0) & (r < ROPE_DIM)
    src = np.where(rot, idx - r + _rope_partner(np.clip(r, 0, ROPE_DIM - 1)), idx)
    return jnp.where(jnp.asarray(rot), a[..., src], jnp.zeros((), a.dtype))


def _head_norm_rope(x, xp, cg, sg, out_scale):
    ms = jnp.sum(x * x, axis=-1, keepdims=True) * (1.0 / QK_DIM)
    return (x * cg + xp * sg) * (lax.rsqrt(ms + EPS) * out_scale)


def _kv_kernel(kva_ref, kr_ref, kr2_ref, ng_ref, wk_ref, wv_ref, vone_ref, kg_ref, kg2_ref,
               cos_ref, sin_ref, ckv_ref, k_ref, v_ref, *, normalize):
    kva = kva_ref[...].astype(F32)
    ckv = _rms(kva, KV_RANK) * ng_ref[...] if normalize else kva
    if normalize:
        ckv_ref[...] = ckv
    cb = ckv.astype(MXU_DTYPE)
    kfull = jnp.dot(cb, wk_ref[...], preferred_element_type=F32)
    kr = kr_ref[...].astype(F32)
    cg = cos_ref[...] * kg_ref[...]
    sg = sin_ref[...] * kg2_ref[...]
    xp = kr2_ref[...].astype(F32)
    for h in range(N_HEADS):
        sl = slice(h * HEAD_PAD, (h + 1) * HEAD_PAD)
        k_ref[:, sl] = _head_norm_rope(kfull[:, sl] + kr, xp, cg, sg, 1.0).astype(k_ref.dtype)
    v = jnp.dot(cb, wv_ref[...], preferred_element_type=F32) + vone_ref[...]
    v_ref[...] = v.astype(v_ref.dtype)


def _kv_proj(src, kv_blk, kr_src, kr_blk, kr2_blk, norm_g, wk, wv, vone, kg, kg2, cos, sin,
             table_map, tl, normalize):
    nt = src.shape[0]
    const = lambda i: (0, 0)
    row = lambda i: (i, 0)
    hw = N_HEADS * HEAD_PAD
    out_shape = [jax.ShapeDtypeStruct((nt, KV_RANK), F32),
                 jax.ShapeDtypeStruct((nt, hw), MXU_DTYPE),
                 jax.ShapeDtypeStruct((nt, hw), MXU_DTYPE)]
    out_specs = [pl.BlockSpec((tl, KV_RANK), row),
                 pl.BlockSpec((tl, hw), row),
                 pl.BlockSpec((tl, hw), row)]
    n_in = 11
    kern = functools.partial(_kv_kernel, normalize=normalize)
    if not normalize:
        out_shape, out_specs = out_shape[1:], out_specs[1:]
        kern = lambda *refs: _kv_kernel(*refs[:n_in], None, *refs[n_in:], normalize=False)
    return pl.pallas_call(
        kern,
        grid=(nt // tl,),
        in_specs=[pl.BlockSpec((tl, KV_RANK), lambda i: (i, kv_blk)),
                  pl.BlockSpec((tl, HEAD_PAD), lambda i: (i, kr_blk)),
                  pl.BlockSpec((tl, HEAD_PAD), lambda i: (i, kr2_blk)),
                  pl.BlockSpec((1, KV_RANK), const),
                  pl.BlockSpec((KV_RANK, hw), const),
                  pl.BlockSpec((KV_RANK, hw), const),
                  pl.BlockSpec((1, hw), const),
                  pl.BlockSpec((1, HEAD_PAD), const),
                  pl.BlockSpec((1, HEAD_PAD), const),
                  pl.BlockSpec((tl, HEAD_PAD), table_map),
                  pl.BlockSpec((tl, HEAD_PAD), table_map)],
        out_specs=out_specs,
        out_shape=out_shape,
        compiler_params=_cp(("parallel",)),
        name="kv_proj" if normalize else "kv_proj_cache",
    )(src, kr_src, kr_src, norm_g, wk, wv, vone, kg, kg2, cos, sin)


def _q_kernel(qa_ref, ng_ref, wq_ref, wq2_ref, qg_ref, qg2_ref, cos_ref, sin_ref, q_ref):
    qn = (_rms(qa_ref[...].astype(F32), Q_RANK) * ng_ref[...]).astype(MXU_DTYPE)
    qfull = jnp.dot(qn, wq_ref[...], preferred_element_type=F32)
    qperm = jnp.dot(qn, wq2_ref[...], preferred_element_type=F32)
    cg = cos_ref[...] * qg_ref[...]
    sg = sin_ref[...] * qg2_ref[...]
    for h in range(N_HEADS):
        sl = slice(h * HEAD_PAD, (h + 1) * HEAD_PAD)
        q_ref[:, sl] = _head_norm_rope(qfull[:, sl], qperm[:, sl], cg, sg, SOFTMAX_C).astype(q_ref.dtype)


def _q_proj(proj, norm_g, wq, wq2, qg, qg2, cos, sin, table_map, tl):
    nt = proj.shape[0]
    const = lambda i: (0, 0)
    hw = N_HEADS * HEAD_PAD
    return pl.pallas_call(
        _q_kernel,
        grid=(nt // tl,),
        in_specs=[pl.BlockSpec((tl, Q_PAD), lambda i: (i, OFF_Q // Q_PAD)),
                  pl.BlockSpec((1, Q_PAD), const),
                  pl.BlockSpec((Q_PAD, hw), const),
                  pl.BlockSpec((Q_PAD, hw), const),
                  pl.BlockSpec((1, HEAD_PAD), const),
                  pl.BlockSpec((1, HEAD_PAD), const),
                  pl.BlockSpec((tl, HEAD_PAD), table_map),
                  pl.BlockSpec((tl, HEAD_PAD), table_map)],
        out_specs=pl.BlockSpec((tl, hw), lambda i: (i, 0)),
        out_shape=jax.ShapeDtypeStruct((nt, hw), MXU_DTYPE),
        compiler_params=_cp(("parallel",)),
        name="q_proj",
    )(proj, norm_g, wq, wq2, qg, qg2, cos, sin)


def _rope_tables(tl, dec_seq):
    a = ROPE_DIM // 2
    t = np.arange(dec_seq)
    pos = np.stack([t // GRID_W, t % GRID_W]).astype(np.float32)
    inv = np.power(np.float32(ROPE_BASE), -np.arange(0, a, 2, dtype=np.float32) / np.float32(a))
    ang = pos[:, :, None] * inv.astype(np.float32)
    cos = np.ones((tl + dec_seq, HEAD_PAD), np.float32)
    sin = np.zeros((tl + dec_seq, HEAD_PAD), np.float32)
    hq = a // 2
    for i in range(2):
        lo = NOPE_DIM + i * a
        cos[tl:, lo:lo + hq] = np.cos(ang[i])
        cos[tl:, lo + hq:lo + a] = np.cos(ang[i])
        sin[tl:, lo:lo + hq] = -np.sin(ang[i])
        sin[tl:, lo + hq:lo + a] = np.sin(ang[i])
    return jnp.asarray(cos), jnp.asarray(sin)


def _attn_kernel(*refs, has_ctx, chunk):
    if has_ctx:
        q_ref, k_ref, v_ref, kc_ref, vc_ref, o_ref = refs
    else:
        q_ref, k_ref, v_ref, o_ref = refs
    nt_dims = (((1,), (1,)), ((), ()))
    tq = q_ref.shape[0]
    length = k_ref.shape[0]
    chunks = [(k_ref, v_ref, j, min(chunk, length - j)) for j in range(0, length, chunk)]
    if has_ctx:
        chunks = [(kc_ref, vc_ref, 0, kc_ref.shape[0])] + chunks
    heads = [slice(h * HEAD_PAD, (h + 1) * HEAD_PAD) for h in range(q_ref.shape[1] // HEAD_PAD)]
    qs = [q_ref[:, hs] for hs in heads]
    m = [jnp.full((tq, 1), -jnp.inf, F32) for _ in heads]
    acc = [jnp.zeros((tq, HEAD_PAD), F32) for _ in heads]
    for kr, vr, start, size in chunks:
        rows = pl.ds(start, size)
        for h, hs in enumerate(heads):
            s = lax.dot_general(qs[h], kr[rows, hs], nt_dims, preferred_element_type=F32)
            m_new = jnp.maximum(m[h], jnp.max(s, axis=-1, keepdims=True))
            p = jnp.exp2(s - m_new).astype(MXU_DTYPE)
            acc[h] = acc[h] * jnp.exp2(m[h] - m_new) + jnp.dot(p, vr[rows, hs],
                                                               preferred_element_type=F32)
            m[h] = m_new
    outs = [a[:, 0:V_DIM] / a[:, V_DIM:V_DIM + 1] for a in acc]
    o_ref[...] = jnp.concatenate(outs, axis=-1).astype(o_ref.dtype)


def _attention(q, k, v, out_rows, row_off, bsz, length, tq, nh, ctx=None):
    assert row_off % length == 0 and length % tq == 0
    qb0 = row_off // tq
    kb0 = row_off // length
    nq = length // tq
    hp = N_HEADS // nh
    in_specs = [pl.BlockSpec((tq, nh * HEAD_PAD), lambda b, h, i: (qb0 + b * nq + i, h)),
                pl.BlockSpec((length, nh * HEAD_PAD), lambda b, h, i: (kb0 + b, h)),
                pl.BlockSpec((length, nh * HEAD_PAD), lambda b, h, i: (kb0 + b, h))]
    args = [q, k, v]
    if ctx is not None:
        kc, vc, past = ctx
        in_specs += [pl.BlockSpec((past, nh * HEAD_PAD), lambda b, h, i: (b, h)),
                     pl.BlockSpec((past, nh * HEAD_PAD), lambda b, h, i: (b, h))]
        args += [kc, vc]
    return pl.pallas_call(
        functools.partial(_attn_kernel, has_ctx=ctx is not None, chunk=min(length, 2048)),
        grid=(bsz, hp, nq),
        in_specs=in_specs,
        out_specs=pl.BlockSpec((tq, nh * V_DIM), lambda b, h, i: (b * nq + i, h)),
        out_shape=jax.ShapeDtypeStruct((out_rows, N_HEADS * V_DIM), MXU_DTYPE),
        compiler_params=_cp(("parallel", "parallel", "arbitrary")),
        name="attention_ctx" if ctx is None else "attention_latent",
    )(*args)


def _merge_kernel(pc_ref, pl_ref, hc_ref, hl_ref, ac_ref, al_ref, g0_ref, g1_ref, g2_ref, x_ref, mod_ref,
                  wp_ref, wh_ref, wa_ref, wo_ref, o_ref, *, n_ctx_tiles):
    is_ctx = pl.program_id(0) < n_ctx_tiles
    pick = lambda c_ref, l_ref: jnp.where(is_ctx, c_ref[...], l_ref[...])
    yp = jnp.dot(pick(pc_ref, pl_ref), wp_ref[...], preferred_element_type=F32)
    yh = jnp.dot(pick(hc_ref, hl_ref), wh_ref[...], preferred_element_type=F32)
    ya = jnp.dot(pick(ac_ref, al_ref), wa_ref[...], preferred_element_type=F32)
    gate = lambda ref: _sigmoid(ref[...].astype(F32))
    merged = gate(g0_ref) * yp + gate(g1_ref) * yh + gate(g2_ref) * ya
    mix = jnp.dot(merged.astype(MXU_DTYPE), wo_ref[...], preferred_element_type=F32)
    o_ref[...] = x_ref[...] + mod_ref[0][2:3] * mix


def _merge(p, hy, att, proj, x, mod, wp, wh, wa, wo, tm, ctx_rows, dec_seq):
    nt, d = x.shape
    row = lambda i: (i, 0)
    const = lambda i: (0, 0)
    g0 = OFF_G // d
    nc = ctx_rows // tm
    pair = lambda w: [pl.BlockSpec((tm, w), lambda i: (jnp.minimum(i, nc - 1), 0)),
                      pl.BlockSpec((tm, w), lambda i: (jnp.maximum(i - nc, 0), 0))]
    return pl.pallas_call(
        functools.partial(_merge_kernel, n_ctx_tiles=nc),
        grid=(nt // tm,),
        in_specs=pair(POOL_WIDTH) + pair(HY_WIDTH) + pair(N_HEADS * V_DIM) + [
                  pl.BlockSpec((tm, d), lambda i: (i, g0)),
                  pl.BlockSpec((tm, d), lambda i: (i, g0 + 1)),
                  pl.BlockSpec((tm, d), lambda i: (i, g0 + 2)),
                  pl.BlockSpec((tm, d), row),
                  pl.BlockSpec((1, 6, d), _mod_row_map(tm, ctx_rows, dec_seq)),
                  pl.BlockSpec((POOL_WIDTH, d), const),
                  pl.BlockSpec((HY_WIDTH, d), const),
                  pl.BlockSpec((N_HEADS * V_DIM, d), const),
                  pl.BlockSpec((d, d), const)],
        out_specs=pl.BlockSpec((tm, d), row),
        out_shape=jax.ShapeDtypeStruct((nt, d), F32),
        compiler_params=_cp(("parallel",)),
        name="merge",
    )(*p, *hy, *att, proj, proj, proj, x, mod, wp, wh, wa, wo)


def _norm2(x_ref, mod_ref, g_ref):
    m = mod_ref[0]
    return (_rms(x_ref[...], D_MODEL) * g_ref[...]) * (1.0 + m[4:5]) + m[3:4]


def _ffn_kernel(x_ref, mod_ref, g_ref, w1_ref, w3_ref, w2_ref, o_ref, *, sub):
    m = mod_ref[0]
    g = g_ref[...]
    for j in range(x_ref.shape[0] // sub):
        rs = pl.ds(j * sub, sub)
        x = x_ref[rs, :]
        h = ((_rms(x, D_MODEL) * g) * (1.0 + m[4:5]) + m[3:4]).astype(MXU_DTYPE)
        a = jnp.dot(h, w1_ref[...], preferred_element_type=F32)
        b = jnp.dot(h, w3_ref[...], preferred_element_type=F32)
        f = jnp.dot((_silu(a) * b).astype(MXU_DTYPE), w2_ref[...], preferred_element_type=F32)
        o_ref[rs, :] = x + m[5:6] * f


def _ffn_dense(x, mod, norm_g, w1, w3, w2, tm, ctx_rows, dec_seq):
    nt, d = x.shape
    ff = w1.shape[1]
    resident = lambda shape: pl.BlockSpec(shape, lambda i: (0, 0), pipeline_mode=pl.Buffered(1))
    return pl.pallas_call(
        functools.partial(_ffn_kernel, sub=min(tm, 256)),
        grid=(nt // tm,),
        in_specs=[pl.BlockSpec((tm, d), lambda i: (i, 0)),
                  pl.BlockSpec((1, 6, d), _mod_row_map(tm, ctx_rows, dec_seq)),
                  pl.BlockSpec((1, d), lambda i: (0, 0)),
                  resident((d, ff)), resident((d, ff)), resident((ff, d))],
        out_specs=pl.BlockSpec((tm, d), lambda i: (i, 0)),
        out_shape=jax.ShapeDtypeStruct((nt, d), F32),
        compiler_params=_cp(("parallel",)),
        name="ffn_dense",
    )(x, mod, norm_g, w1, w3, w2)


def _router_kernel(x_ref, mod_ref, g_ref, r_ref, h_ref, sel_ref):
    h = _norm2(x_ref, mod_ref, g_ref)
    h_ref[...] = h
    logits = _mm_f32(h, r_ref[...])
    lane = lax.broadcasted_iota(jnp.int32, logits.shape, 1).astype(F32)
    neg = jnp.float32(-jnp.inf)
    logits = jnp.where(lane < N_EXPERTS, logits, neg)
    m1 = jnp.max(logits, axis=-1, keepdims=True)
    i1 = jnp.min(jnp.where(logits == m1, lane, float(LANE)), axis=-1, keepdims=True)
    rest = jnp.where(lane == i1, neg, logits)
    m2 = jnp.max(rest, axis=-1, keepdims=True)
    i2 = jnp.min(jnp.where(rest == m2, lane, float(LANE)), axis=-1, keepdims=True)
    e2 = jnp.exp(m2 - m1)
    w_top = 1.0 / (1.0 + e2)
    sel_ref[...] = (jnp.where(lane == 0.0, i1, 0.0) + jnp.where(lane == 1.0, i2, 0.0)
                    + jnp.where(lane == 2.0, w_top, 0.0) + jnp.where(lane == 3.0, e2 * w_top, 0.0))


def _router(x, mod, norm_g, router, tm, ctx_rows, dec_seq):
    nt, d = x.shape
    rp = jnp.pad(router, ((0, 0), (0, LANE - N_EXPERTS)))
    return pl.pallas_call(
        _router_kernel,
        grid=(nt // tm,),
        in_specs=[pl.BlockSpec((tm, d), lambda i: (i, 0)),
                  pl.BlockSpec((1, 6, d), _mod_row_map(tm, ctx_rows, dec_seq)),
                  pl.BlockSpec((1, d), lambda i: (0, 0)),
                  pl.BlockSpec((d, LANE), lambda i: (0, 0))],
        out_specs=[pl.BlockSpec((tm, d), lambda i: (i, 0)),
                   pl.BlockSpec((tm, LANE), lambda i: (i, 0))],
        out_shape=[jax.ShapeDtypeStruct((nt, d), F32),
                   jax.ShapeDtypeStruct((nt, LANE), F32)],
        compiler_params=_cp(("parallel",)),
        name="router",
    )(x, mod, norm_g, rp)


def _dispatch_plan(sel, rows_per_tile):
    nt = sel.shape[0]
    n_pairs = 2 * nt
    r = rows_per_tile
    n_tiles = n_pairs // r + N_EXPERTS
    e_p = sel[:, 0:2].astype(jnp.int32).T.reshape(n_pairs)
    experts = jnp.arange(N_EXPERTS, dtype=jnp.int32)
    onehot = (e_p[:, None] == experts[None, :]).astype(jnp.int32)
    csum = jnp.cumsum(onehot, axis=0)
    rank = jnp.sum((csum - onehot) * onehot, axis=1)
    counts = csum[-1]
    padded = (counts + r - 1) // r * r
    ends = jnp.cumsum(padded)
    starts = ends - padded
    slot_p = jnp.sum(onehot * starts[None, :], axis=1) + rank
    tile_start = jnp.arange(n_tiles, dtype=jnp.int32) * r
    tile_expert = jnp.minimum(jnp.sum((tile_start[:, None] >= ends[None, :]).astype(jnp.int32), axis=1),
                              N_EXPERTS - 1)
    run_end = jnp.sum((tile_expert[:, None] == experts[None, :]) * (starts + counts)[None, :], axis=1)
    tile_rows = jnp.clip(run_end - tile_start, 0, r)
    return tile_expert, tile_rows, slot_p


def _row_copies(src, src_rows, dst, dst_rows, sem, n):
    return [pltpu.make_async_copy(src.at[pl.ds(src_rows(i), 1), :], dst.at[pl.ds(dst_rows(i), 1), :], sem)
            for i in range(n)]


def _start_and_wait(copies):
    for i, cp in enumerate(copies):
        cp.start(priority=i % 2)
    for cp in copies:
        cp.wait()


def _dispatch_kernel(idx_ref, h_ref, xs_hbm, sem, *, n):
    copies = _row_copies(h_ref, lambda i: i, xs_hbm, lambda i: idx_ref[0, 0, i], sem, n)
    _start_and_wait(copies)


def _moe_dispatch(h, slot_p, n_slots, pb):
    nt, d = h.shape
    nb = nt // pb
    n_blocks = slot_p.shape[0] // pb
    return pl.pallas_call(
        functools.partial(_dispatch_kernel, n=pb),
        grid=(n_blocks,),
        in_specs=[pl.BlockSpec((1, 1, pb), lambda i: (i, 0, 0), memory_space=pltpu.SMEM),
                  pl.BlockSpec((pb, d), lambda i: (i % nb, 0))],
        out_specs=pl.BlockSpec(memory_space=pl.ANY),
        out_shape=jax.ShapeDtypeStruct((n_slots, d), F32),
        scratch_shapes=[pltpu.SemaphoreType.DMA(())],
        compiler_params=_cp(("arbitrary",)),
        name="moe_dispatch",
    )(slot_p.reshape(n_blocks, 1, pb), h)


def _moe_kernel(te_ref, tr_ref, xs_ref, w13_ref, w2_ref, ys_ref, *, rows, sub):
    n_rows = tr_ref[pl.program_id(0)]
    ff = w2_ref.shape[1]

    @pl.when(n_rows > 0)
    def _():
        for j in range(rows // sub):
            rs = pl.ds(j * sub, sub)
            occupied = j * sub + lax.broadcasted_iota(jnp.int32, (sub, 1), 0) < n_rows
            xb = jnp.where(occupied, xs_ref[rs, :], 0.0).astype(MXU_DTYPE)
            ab = jnp.dot(xb, w13_ref[0], preferred_element_type=F32)
            gated = (_silu(ab[:, 0:ff]) * ab[:, ff:2 * ff]).astype(MXU_DTYPE)
            ys_ref[rs, :] = jnp.dot(gated, w2_ref[0], preferred_element_type=F32)

    @pl.when(n_rows == 0)
    def _():
        ys_ref[...] = jnp.zeros_like(ys_ref)


def _ffn_moe(xs, tile_expert, tile_rows, w13, w2, rows):
    n_slots, d = xs.shape
    _, ff, _ = w2.shape
    wspec = lambda shape: pl.BlockSpec((1,) + shape, lambda t, te, tr: (te[t], 0, 0))
    grid_spec = pltpu.PrefetchScalarGridSpec(
        num_scalar_prefetch=2,
        grid=(n_slots // rows,),
        in_specs=[pl.BlockSpec((rows, d), lambda t, te, tr: (t, 0)),
                  wspec((d, 2 * ff)), wspec((ff, d))],
        out_specs=pl.BlockSpec((rows, d), lambda t, te, tr: (t, 0)))
    return pl.pallas_call(
        functools.partial(_moe_kernel, rows=rows, sub=min(rows, 256)),
        grid_spec=grid_spec,
        out_shape=jax.ShapeDtypeStruct((n_slots, d), F32),
        compiler_params=_cp(("arbitrary",)),
        name="ffn_moe",
    )(tile_expert, tile_rows, xs, w13, w2)


def _combine_kernel(idx0_ref, idx1_ref, x_ref, mod_ref, sel_ref, ys_hbm, o_ref, buf, sem, *, n):
    copies = (_row_copies(ys_hbm, lambda i: idx0_ref[0, 0, i], buf.at[0], lambda i: i, sem, n)
              + _row_copies(ys_hbm, lambda i: idx1_ref[0, 0, i], buf.at[1], lambda i: i, sem, n))
    _start_and_wait(copies)
    sel = sel_ref[...]
    f = sel[:, 2:3] * buf[0] + sel[:, 3:4] * buf[1]
    o_ref[...] = x_ref[...] + mod_ref[0][5:6] * f


def _moe_combine(x, mod, sel, ys, slot_p, tm, ctx_rows, dec_seq):
    nt, d = x.shape
    nb = nt // tm
    idx = slot_p.reshape(2 * nb, 1, tm)
    row = lambda i: (i, 0)
    return pl.pallas_call(
        functools.partial(_combine_kernel, n=tm),
        grid=(nb,),
        in_specs=[pl.BlockSpec((1, 1, tm), lambda i: (i, 0, 0), memory_space=pltpu.SMEM),
                  pl.BlockSpec((1, 1, tm), lambda i: (nb + i, 0, 0), memory_space=pltpu.SMEM),
                  pl.BlockSpec((tm, d), row),
                  pl.BlockSpec((1, 6, d), _mod_row_map(tm, ctx_rows, dec_seq)),
                  pl.BlockSpec((tm, LANE), row),
                  pl.BlockSpec(memory_space=pl.ANY)],
        out_specs=pl.BlockSpec((tm, d), row),
        out_shape=jax.ShapeDtypeStruct((nt, d), F32),
        scratch_shapes=[pltpu.VMEM((2, tm, d), F32), pltpu.SemaphoreType.DMA(())],
        compiler_params=_cp(("arbitrary",)),
        name="moe_combine",
    )(idx, idx, x, mod, sel, ys)


def _prep_w_in(w_in):
    depth, d, _ = w_in.shape
    c1 = POOL_WIDTH
    c2 = c1 + 3 * HY_WIDTH
    c3 = c2 + Q_RANK
    c4 = c3 + KV_RANK
    c5 = c4 + ROPE_DIM
    z = lambda n: jnp.zeros((depth, d, n), w_in.dtype)
    kr_group = jnp.concatenate([z(NOPE_DIM), w_in[:, :, c4:c5], z(HEAD_PAD - QK_DIM)], axis=-1)
    cols = [w_in[:, :, c1:c2], w_in[:, :, 0:c1], w_in[:, :, c3:c4],
            kr_group, _partner_lanes(kr_group),
            w_in[:, :, c2:c3], z(Q_PAD - Q_RANK), w_in[:, :, c5:]]
    out = jnp.concatenate(cols, axis=-1).astype(MXU_DTYPE)
    assert out.shape[-1] == IN_PAD
    return out


def _head_pad_cols(w, width):
    lead = w.shape[:-1]
    w = w.reshape(*lead, N_HEADS, width)
    w = jnp.pad(w, [(0, 0)] * len(lead) + [(0, 0), (0, HEAD_PAD - width)])
    return w.reshape(*lead, N_HEADS * HEAD_PAD)


def _pad_last(a, n):
    return jnp.pad(a, [(0, 0)] * (a.ndim - 1) + [(0, n - a.shape[-1])])


def kernel(x_prompt, x_sample, cache_ckv, cache_krope, c, c_ctx, w_mod, b_mod, norm1_g, norm2_g, w_in, pool_w, pool_scale, pool_out, hy_conv_w, hy_conv_b, hy_w1, hy_b1, hy_w2, hy_b2, hy_w3, hy_freq, hy_decay, hy_bias, hy_out, q_norm_g, w_qb, kv_norm_g, w_kvb, qk_q_g, qk_k_g, mla_out, w_out, ffn_w1, ffn_w3, ffn_w2, moe_router, moe_w1, moe_w3, moe_w2):
    batch, seq, d = x_prompt.shape
    dec_batch, dec_seq, _ = x_sample.shape
    depth = w_mod.shape[0]
    past = cache_ckv.shape[2]
    ctx_rows = batch * seq
    lat_rows = dec_batch * dec_seq
    nt = ctx_rows + lat_rows
    tl = 256
    assert seq % tl == 0 and dec_seq % tl == 0 and ctx_rows % dec_seq == 0 and past % tl == 0
    tm = min(1024, math.gcd(ctx_rows, dec_seq))
    n_ctx_tiles = ctx_rows // tl
    bf = lambda a: a.astype(MXU_DTYPE)

    w_in_p = _prep_w_in(w_in)
    eye = jnp.eye(POOL_GROUPS, dtype=pool_w.dtype)
    gc = POOL_WIDTH // POOL_GROUPS
    pool_bd = bf(jnp.einsum('lgcd,gh->lgchd', pool_w, eye).reshape(depth, POOL_WIDTH, POOL_WIDTH))
    kvb = w_kvb.reshape(depth, KV_RANK, N_HEADS, NOPE_DIM + V_DIM)
    wk = bf(_head_pad_cols(kvb[..., :NOPE_DIM].reshape(depth, KV_RANK, N_HEADS * NOPE_DIM), NOPE_DIM))
    wv = bf(_head_pad_cols(kvb[..., NOPE_DIM:].reshape(depth, KV_RANK, N_HEADS * V_DIM), V_DIM))
    vone = jnp.tile((jnp.arange(HEAD_PAD) == V_DIM).astype(F32), N_HEADS)[None, :]
    wq_f = jnp.pad(_head_pad_cols(w_qb, QK_DIM), ((0, 0), (0, Q_PAD - Q_RANK), (0, 0)))
    wq, wq2 = bf(wq_f), bf(_partner_lanes(wq_f))
    q_norm_p = _pad_last(q_norm_g, Q_PAD)
    qg = _pad_last(qk_q_g, HEAD_PAD)
    kg = _pad_last(qk_k_g, HEAD_PAD)
    qg2, kg2 = _partner_lanes(qg), _partner_lanes(kg)
    pool_out_b, hy_out_b, mla_out_b, w_out_b = bf(pool_out), bf(hy_out), bf(mla_out), bf(w_out)
    ffn_w1_b, ffn_w3_b, ffn_w2_b = bf(ffn_w1), bf(ffn_w3), bf(ffn_w2)
    moe_w13_b, moe_w2_b = bf(jnp.concatenate([moe_w1, moe_w3], axis=-1)), bf(moe_w2)

    tp = math.gcd(512, math.gcd(ctx_rows, math.gcd(dec_seq, dec_batch * past)))
    cos_t, sin_t = _rope_tables(tp, dec_seq)
    n_ctx_tp = ctx_rows // tp
    n_lat_tp = dec_seq // tp
    table_map = lambda i: (jnp.where(i < n_ctx_tp, 0, 1 + (i - n_ctx_tp) % n_lat_tp), 0)
    ident_map = lambda i: (0, 0)
    mats_ctx = _dft_matrices(seq)
    mats_lat = _dft_matrices(dec_seq)

    n_cond = 1 + dec_batch
    cond_rows = -(-n_cond // SUBLANE) * SUBLANE
    cond = jnp.concatenate([c_ctx[None, :], c, jnp.zeros((cond_rows - n_cond, d), c.dtype)], axis=0)
    mod_all = _modulation(cond, w_mod, b_mod)[:, :n_cond].reshape(depth, n_cond, 6, d)
    filt_ctx = _hyena_filters(seq, hy_w1, hy_b1, hy_w2, hy_b2, hy_w3, hy_freq, hy_decay)
    filt_lat = _hyena_filters(dec_seq, hy_w1, hy_b1, hy_w2, hy_b2, hy_w3, hy_freq, hy_decay)
    cache_kr_p = jnp.pad(cache_krope, ((0, 0), (0, 0), (0, 0), (NOPE_DIM, HEAD_PAD - QK_DIM)))
    cache_kr_p = jnp.concatenate([cache_kr_p, _partner_lanes(cache_kr_p)], axis=-1)

    x = jnp.concatenate([x_prompt.reshape(ctx_rows, d), x_sample.reshape(lat_rows, d)], axis=0)
    ckv_list, kr_list = [], []
    for i in range(depth):
        j = i // 2
        mod = mod_all[i]
        r1 = lambda a: a[i].reshape(1, -1)
        proj = _inproj(x, mod, r1(norm1_g), w_in_p[i], tm, ctx_rows, dec_seq)

        local = lambda tile0, n: _local_mixers(proj, hy_conv_w[i], r1(hy_conv_b), pool_bd[i],
                                               r1(pool_scale), tl, tile0, n, n_ctx_tiles, seq, dec_seq)
        vx_c, vxb_c, x0_c, pooled_c = local(0, n_ctx_tiles)
        vx_l, vxb_l, x0_l, pooled_l = local(n_ctx_tiles, lat_rows // tl)
        bias = r1(hy_bias)
        hy_c = _long_conv(vx_c, vxb_c, x0_c, filt_ctx[i], bias, mats_ctx, batch, seq)
        hy_l = _long_conv(vx_l, vxb_l, x0_l, filt_lat[i], bias, mats_lat, dec_batch, dec_seq)

        ckv, k, v = _kv_proj(proj, OFF_KV // KV_RANK, proj, OFF_KROPE // HEAD_PAD, OFF_KROPE2 // HEAD_PAD,
                             r1(kv_norm_g), wk[i], wv[i], vone, r1(kg), r1(kg2), cos_t, sin_t,
                             table_map, tp, True)
        kc, vc = _kv_proj(cache_ckv[:, i].reshape(dec_batch * past, KV_RANK), 0,
                          cache_kr_p[:, i].reshape(dec_batch * past, 2 * HEAD_PAD), 0, 1,
                          r1(kv_norm_g), wk[i], wv[i], vone, r1(kg), r1(kg2), cos_t, sin_t,
                          ident_map, tp, False)
        q = _q_proj(proj, r1(q_norm_p), wq[i], wq2[i], r1(qg), r1(qg2), cos_t, sin_t, table_map, tp)
        att_c = _attention(q, k, v, ctx_rows, 0, batch, seq, min(seq, 256), 2)
        att_l = _attention(q, k, v, lat_rows, ctx_rows, dec_batch, dec_seq, min(dec_seq, 1024), 2,
                           ctx=(kc, vc, past))
        x = _merge((pooled_c, pooled_l), (hy_c, hy_l), (att_c, att_l), proj, x, mod,
                   pool_out_b[i], hy_out_b[i], mla_out_b[i], w_out_b[i], min(tm, 512), ctx_rows, dec_seq)

        if i % 2 == 1:
            h2, sel = _router(x, mod, r1(norm2_g), moe_router[j], tm, ctx_rows, dec_seq)
            tile_expert, tile_rows, slot_p = _dispatch_plan(sel, MOE_ROWS)
            xs = _moe_dispatch(h2, slot_p, tile_expert.shape[0] * MOE_ROWS, min(tm, 512))
            ys = _ffn_moe(xs, tile_expert, tile_rows, moe_w13_b[j], moe_w2_b[j], MOE_ROWS)
            x = _moe_combine(x, mod, sel, ys, slot_p, min(tm, 512), ctx_rows, dec_seq)
        else:
            x = _ffn_dense(x, mod, r1(norm2_g), ffn_w1_b[j], ffn_w3_b[j], ffn_w2_b[j],
                           tm, ctx_rows, dec_seq)

        ckv_list.append(ckv[:ctx_rows].reshape(batch, seq, KV_RANK))
        kr_list.append(proj[:ctx_rows, OFF_KROPE + NOPE_DIM:OFF_KROPE + QK_DIM].astype(F32)
                       .reshape(batch, seq, ROPE_DIM))

    y_p = x[:ctx_rows].reshape(batch, seq, d)
    y_s = x[ctx_rows:].reshape(dec_batch, dec_seq, d)
    return (y_p, y_s, jnp.stack(ckv_list, axis=1), jnp.stack(kr_list, axis=1))
```

```python
import functools
import math

import numpy as np
import jax
import jax.numpy as jnp
from jax import lax
from jax.experimental import pallas as pl
from jax.experimental.pallas import tpu as pltpu

F32 = jnp.float32
MXU_DTYPE = jnp.bfloat16
HIGHEST = lax.Precision.HIGHEST

D_MODEL = 1024
GRID_W = 64
EPS = 1e-6
POOL_WIDTH = 256
POOL_GROUPS = 4
POOL_WINDOWS = (2, 4, 8, 16)
HY_WIDTH = 256
HY_SHORT = 3
HY_BANDS = 8
HY_EMB = 1 + 2 * HY_BANDS
HY_FFN = 64
N_HEADS = 8
Q_RANK = 384
KV_RANK = 256
NOPE_DIM = 64
ROPE_DIM = 32
V_DIM = 64
QK_DIM = NOPE_DIM + ROPE_DIM
ROPE_BASE = 10000.0
N_EXPERTS = 8

LANE = 128
SUBLANE = 8
HALO = 16
HEAD_PAD = LANE
Q_PAD = 512
MOE_ROWS = 512
VMEM_LIMIT = 56 * 1024 * 1024
SOFTMAX_C = (QK_DIM ** -0.5) * math.log2(math.e)

OFF_HY = 0
OFF_POOL = 3 * HY_WIDTH
OFF_KV = OFF_POOL + POOL_WIDTH
OFF_KROPE = OFF_KV + KV_RANK
OFF_KROPE2 = OFF_KROPE + HEAD_PAD
OFF_Q = 1536
OFF_G = OFF_Q + Q_PAD
IN_PAD = OFF_G + 3 * D_MODEL


def _cp(sem, vmem=VMEM_LIMIT):
    return pltpu.CompilerParams(dimension_semantics=sem, vmem_limit_bytes=vmem)


def _mm(a, b):
    return jnp.dot(a.astype(MXU_DTYPE), b.astype(MXU_DTYPE), preferred_element_type=F32)


def _mm_f32(a, b):
    return jnp.dot(a, b, preferred_element_type=F32, precision=HIGHEST)


def _sigmoid(x):
    return 1.0 / (1.0 + jnp.exp(-x))


def _silu(x):
    return x * _sigmoid(x)


def _rms(x, n):
    ms = jnp.sum(x * x, axis=-1, keepdims=True) * (1.0 / n)
    return x * lax.rsqrt(ms + EPS)


def _mod_row_map(tm, ctx_rows, dec_seq):
    def index_map(i, *_):
        row0 = i * tm
        return (jnp.where(row0 < ctx_rows, 0, 1 + (row0 - ctx_rows) // dec_seq), 0, 0)
    return index_map


def _mod_kernel(c_ref, w_ref, b_ref, o_ref):
    o_ref[0] = _mm_f32(_silu(c_ref[...]), w_ref[0]) + b_ref[0]


def _modulation(cond, w_mod, b_mod):
    depth, d, n6 = w_mod.shape
    r = cond.shape[0]
    tn = 1024
    return pl.pallas_call(
        _mod_kernel,
        grid=(depth, n6 // tn),
        in_specs=[pl.BlockSpec((r, d), lambda l, j: (0, 0)),
                  pl.BlockSpec((1, d, tn), lambda l, j: (l, 0, j)),
                  pl.BlockSpec((1, 1, tn), lambda l, j: (l, 0, j))],
        out_specs=pl.BlockSpec((1, r, tn), lambda l, j: (l, 0, j)),
        out_shape=jax.ShapeDtypeStruct((depth, r, n6), F32),
        compiler_params=_cp(("parallel", "parallel")),
        name="modulation",
    )(cond, w_mod, b_mod.reshape(depth, 1, n6))


def _inproj_kernel(x_ref, mod_ref, g_ref, w_ref, o_ref, *, sub):
    m = mod_ref[0]
    g = g_ref[...]
    for j in range(x_ref.shape[0] // sub):
        rs = pl.ds(j * sub, sub)
        h = ((_rms(x_ref[rs, :], D_MODEL) * g) * (1.0 + m[1:2]) + m[0:1]).astype(MXU_DTYPE)
        o_ref[rs, :] = jnp.dot(h, w_ref[...], preferred_element_type=F32).astype(o_ref.dtype)


def _inproj(x, mod, norm_g, w, tm, ctx_rows, dec_seq):
    nt, d = x.shape
    n = w.shape[1]
    return pl.pallas_call(
        functools.partial(_inproj_kernel, sub=min(tm, 256)),
        grid=(nt // tm,),
        in_specs=[pl.BlockSpec((tm, d), lambda i: (i, 0)),
                  pl.BlockSpec((1, 6, d), _mod_row_map(tm, ctx_rows, dec_seq)),
                  pl.BlockSpec((1, d), lambda i: (0, 0)),
                  pl.BlockSpec((d, n), lambda i: (0, 0), pipeline_mode=pl.Buffered(1))],
        out_specs=pl.BlockSpec((tm, n), lambda i: (i, 0)),
        out_shape=jax.ShapeDtypeStruct((nt, n), MXU_DTYPE),
        compiler_params=_cp(("parallel",)),
        name="inproj",
    )(x, mod, norm_g, w)


def _local_kernel(u_ref, prev_ref, next_ref, cw_ref, cb_ref, pw_ref, ps_ref,
                  vx_ref, vxb_ref, x0_ref, p_ref, ext_ref, *, tl, tile0, n_ctx_tiles, seq, dec_seq):
    i = pl.program_id(0) + tile0
    row0 = i * tl
    ctx = i < n_ctx_tiles
    p0 = jnp.where(ctx, row0 % seq, (row0 - n_ctx_tiles * tl) % dec_seq)
    length = jnp.where(ctx, seq, dec_seq)
    first = p0 == 0
    last = p0 + tl == length

    ext_ref[0:SUBLANE, :] = jnp.where(first, 0.0, prev_ref[HALO - SUBLANE:HALO, :].astype(F32))
    ext_ref[SUBLANE:SUBLANE + tl, :] = u_ref[...].astype(F32)
    ext_ref[SUBLANE + tl:, :] = jnp.where(last, 0.0, next_ref[0:SUBLANE, :].astype(F32))

    hw = 3 * HY_WIDTH
    cw = cw_ref[...]
    z = (cb_ref[...]
         + ext_ref[pl.ds(SUBLANE - 1, tl), 0:hw] * cw[0:1]
         + ext_ref[pl.ds(SUBLANE, tl), 0:hw] * cw[1:2]
         + ext_ref[pl.ds(SUBLANE + 1, tl), 0:hw] * cw[2:3])
    x0 = z[:, 0:HY_WIDTH]
    vx = z[:, 2 * HY_WIDTH:3 * HY_WIDTH] * z[:, HY_WIDTH:2 * HY_WIDTH]
    vx_ref[...] = vx
    vxb_ref[...] = vx.astype(vxb_ref.dtype)
    x0_ref[...] = x0

    def e(j):
        return ext_ref[pl.ds(SUBLANE + j, tl), hw:hw + POOL_WIDTH]

    u = e(0)
    s2 = e(-1) + u
    s4 = s2 + e(-2) + e(1)
    s8 = s4 + e(-4) + e(-3) + e(2) + e(3)
    s16 = s8 + e(-8) + e(-7) + e(-6) + e(-5) + e(4) + e(5) + e(6) + e(7)
    lane = lax.broadcasted_iota(jnp.int32, (tl, POOL_WIDTH), 1)
    gc = POOL_WIDTH // POOL_GROUPS
    ssum = jnp.where(lane < gc, s2, jnp.where(lane < 2 * gc, s4, jnp.where(lane < 3 * gc, s8, s16)))
    half = jnp.where(lane < gc, 1, jnp.where(lane < 2 * gc, 2, jnp.where(lane < 3 * gc, 4, 8)))
    t = p0 + lax.broadcasted_iota(jnp.int32, (tl, POOL_WIDTH), 0)
    cnt = jnp.minimum(t + half, length) - jnp.maximum(t - half, 0)
    pooled = ssum / cnt.astype(F32) - u
    p_ref[...] = (_mm(pooled, pw_ref[...]) * ps_ref[...]).astype(p_ref.dtype)


def _local_mixers(proj, conv_w, conv_b, pool_bd, pool_scale, tl, tile0, n_tiles, n_ctx_tiles, seq,
                  dec_seq):
    nt = n_tiles * tl
    width = 3 * HY_WIDTH + POOL_WIDTH
    nb = tl // HALO
    last_blk = proj.shape[0] // HALO - 1
    kern = functools.partial(_local_kernel, tl=tl, tile0=tile0, n_ctx_tiles=n_ctx_tiles, seq=seq,
                             dec_seq=dec_seq)
    row = lambda i: (i, 0)
    const = lambda i: (0, 0)
    return pl.pallas_call(
        kern,
        grid=(n_tiles,),
        in_specs=[pl.BlockSpec((tl, width), lambda i: (i + tile0, 0)),
                  pl.BlockSpec((HALO, width), lambda i: (jnp.maximum((i + tile0) * nb - 1, 0), 0)),
                  pl.BlockSpec((HALO, width),
                               lambda i: (jnp.minimum((i + tile0 + 1) * nb, last_blk), 0)),
                  pl.BlockSpec((HY_SHORT, 3 * HY_WIDTH), const),
                  pl.BlockSpec((1, 3 * HY_WIDTH), const),
                  pl.BlockSpec((POOL_WIDTH, POOL_WIDTH), const),
                  pl.BlockSpec((1, POOL_WIDTH), const)],
        out_specs=[pl.BlockSpec((tl, HY_WIDTH), row)] * 3 + [pl.BlockSpec((tl, POOL_WIDTH), row)],
        out_shape=[jax.ShapeDtypeStruct((nt, HY_WIDTH), F32),
                   jax.ShapeDtypeStruct((nt, HY_WIDTH), MXU_DTYPE),
                   jax.ShapeDtypeStruct((nt, HY_WIDTH), F32),
                   jax.ShapeDtypeStruct((nt, POOL_WIDTH), MXU_DTYPE)],
        scratch_shapes=[pltpu.VMEM((tl + 2 * SUBLANE, width), F32)],
        compiler_params=_cp(("parallel",)),
        name="local_mixers",
    )(proj, proj, proj, conv_w, conv_b, pool_bd, pool_scale)


def _filter_kernel(z_ref, w1_ref, b1_ref, w2_ref, b2_ref, w3_ref, fr_ref, dec_ref, o_ref):
    z = z_ref[...]
    fr = fr_ref[0]
    hdn = jnp.sin(fr * (_mm_f32(z, w1_ref[0]) + b1_ref[0]))
    hdn = jnp.sin(fr * (_mm_f32(hdn, w2_ref[0]) + b2_ref[0]))
    filt = _mm_f32(hdn, w3_ref[0])
    win = jnp.exp(-z[:, 0:1] * jnp.abs(dec_ref[0]))
    o_ref[0] = filt * jnp.concatenate([win, win], axis=-1)


def _hyena_filters(length, w1, b1, w2, b2, w3, freq, decay):
    depth = w1.shape[0]
    t = np.linspace(0.0, 1.0, length, dtype=np.float32)[:, None]
    wpos = (np.float32(2.0 * math.pi / length) * np.arange(length, dtype=np.float32))[:, None]
    bands = np.linspace(1e-4, HY_BANDS - 1, HY_BANDS, dtype=np.float32)[None, :]
    emb_pad = 32
    z = np.zeros((length, emb_pad), np.float32)
    z[:, 0:1] = t
    z[:, 1:1 + HY_BANDS] = np.cos(bands * wpos)
    z[:, 1 + HY_BANDS:HY_EMB] = -np.sin(bands * wpos)
    w1p = jnp.pad(w1, ((0, 0), (0, emb_pad - HY_EMB), (0, 0)))
    tl = min(length, 512)
    lay = lambda l, i: (l, 0, 0)
    r3 = lambda a: a.reshape(depth, 1, a.shape[-1])
    return pl.pallas_call(
        _filter_kernel,
        grid=(depth, length // tl),
        in_specs=[pl.BlockSpec((tl, emb_pad), lambda l, i: (i, 0)),
                  pl.BlockSpec((1, emb_pad, HY_FFN), lay),
                  pl.BlockSpec((1, 1, HY_FFN), lay),
                  pl.BlockSpec((1, HY_FFN, HY_FFN), lay),
                  pl.BlockSpec((1, 1, HY_FFN), lay),
                  pl.BlockSpec((1, HY_FFN, 2 * HY_WIDTH), lay),
                  pl.BlockSpec((1, 1, HY_FFN), lay),
                  pl.BlockSpec((1, 1, HY_WIDTH), lay)],
        out_specs=pl.BlockSpec((1, tl, 2 * HY_WIDTH), lambda l, i: (l, i, 0)),
        out_shape=jax.ShapeDtypeStruct((depth, length, 2 * HY_WIDTH), F32),
        compiler_params=_cp(("parallel", "parallel")),
        name="hyena_filters",
    )(jnp.asarray(z), w1p, r3(b1), w2, r3(b2), w3, r3(freq), r3(decay))


def _dft_matrices(length):
    n = 2 * length
    f = jnp.arange(length, dtype=jnp.int32)[:, None]
    s = jnp.arange(length, dtype=jnp.int32)[None, :]
    blk = 64
    theta = np.float32(2.0 * math.pi / n)
    a_hi = ((f * (s[:, 0:length // blk] * blk)) % n).astype(F32) * theta
    a_lo = ((f * s[:, 0:blk]) % n).astype(F32) * theta
    c_hi, s_hi = jnp.cos(a_hi)[:, :, None], jnp.sin(a_hi)[:, :, None]
    c_lo, s_lo = jnp.cos(a_lo)[:, None, :], jnp.sin(a_lo)[:, None, :]
    cos = (c_hi * c_lo - s_hi * s_lo).reshape(length, length)
    sin = (s_hi * c_lo + c_hi * s_lo).reshape(length, length)
    alt = jnp.where(jnp.arange(length) % 2 == 0, 1.0, -1.0).astype(F32)
    fwd_im = jnp.where(f == 0, alt[None, :], -sin)
    fwd = jnp.concatenate([cos, fwd_im], axis=0)
    inv_re = jnp.where(s == 0, 1.0 / n, (2.0 / n) * cos)
    inv_im = jnp.where(s == 0, alt[:, None] / n, (-2.0 / n) * sin)
    inv = jnp.concatenate([inv_re, inv_im], axis=1)
    return fwd.astype(MXU_DTYPE), inv.astype(MXU_DTYPE)


def _dft_fwd_kernel(g_ref, x_ref, o_ref, *, bg):
    g = g_ref[...]
    for b in range(bg):
        o_ref[b] = jnp.dot(g, x_ref[b], preferred_element_type=F32)


def _dft_fwd(gmat, x, bg):
    bsz, length, cols = x.shape
    tm = min(2 * length, 1024)
    return pl.pallas_call(
        functools.partial(_dft_fwd_kernel, bg=bg),
        grid=(bsz // bg, 2 * length // tm),
        in_specs=[pl.BlockSpec((tm, length), lambda g, m: (m, 0)),
                  pl.BlockSpec((bg, length, cols), lambda g, m: (g, 0, 0), pipeline_mode=pl.Buffered(1))],
        out_specs=pl.BlockSpec((bg, tm, cols), lambda g, m: (g, m, 0)),
        out_shape=jax.ShapeDtypeStruct((bsz, 2 * length, cols), F32),
        compiler_params=_cp(("parallel", "parallel")),
        name="dft_fwd",
    )(gmat, x)


def _dft_inv_kernel(gc_ref, gs_ref, xr_ref, xi_ref, ar_ref, ai_ref, hb0_ref, vx_ref, x0_ref,
                    bias_ref, o_ref, acc_ref, *, bg, tk):
    k = pl.program_id(2)

    @pl.when(k == 0)
    def _():
        acc_ref[...] = jnp.zeros_like(acc_ref)

    c = HY_WIDTH
    ar = ar_ref[...]
    ai = ai_ref[...]
    hb0 = hb0_ref[...]
    f0 = (k * tk + lax.broadcasted_iota(jnp.int32, (tk, c), 0)) == 0
    kr = ar[:, 0:c] + ar[:, c:2 * c] - hb0
    ki = jnp.where(f0, ai[:, 0:c] + ai[:, c:2 * c] - hb0, ai[:, 0:c] - ai[:, c:2 * c])
    gc = gc_ref[...]
    gs = gs_ref[...]
    for b in range(bg):
        xr = xr_ref[b]
        xi = xi_ref[b]
        zr = jnp.where(f0, xr * kr, xr * kr - xi * ki)
        zi = jnp.where(f0, xi * ki, xr * ki + xi * kr)
        acc_ref[b] += (jnp.dot(gc, zr.astype(MXU_DTYPE), preferred_element_type=F32)
                       + jnp.dot(gs, zi.astype(MXU_DTYPE), preferred_element_type=F32))

    @pl.when(k == pl.num_programs(2) - 1)
    def _():
        y = (acc_ref[...] + vx_ref[...] * bias_ref[...]) * x0_ref[...]
        o_ref[...] = y.astype(o_ref.dtype)


def _dft_inv(ginv, xspec, fspec, hb0, vx, x0, bias, bg):
    bsz, length, c = vx.shape
    tm = min(length, 512)
    tk = min(length, 512)
    nk = length // tk
    return pl.pallas_call(
        functools.partial(_dft_inv_kernel, bg=bg, tk=tk),
        grid=(bsz // bg, length // tm, nk),
        in_specs=[pl.BlockSpec((tm, tk), lambda g, m, k: (m, k)),
                  pl.BlockSpec((tm, tk), lambda g, m, k: (m, nk + k)),
                  pl.BlockSpec((bg, tk, c), lambda g, m, k: (g, k, 0)),
                  pl.BlockSpec((bg, tk, c), lambda g, m, k: (g, nk + k, 0)),
                  pl.BlockSpec((tk, 2 * c), lambda g, m, k: (k, 0)),
                  pl.BlockSpec((tk, 2 * c), lambda g, m, k: (nk + k, 0)),
                  pl.BlockSpec((1, c), lambda g, m, k: (0, 0)),
                  pl.BlockSpec((bg, tm, c), lambda g, m, k: (g, m, 0)),
                  pl.BlockSpec((bg, tm, c), lambda g, m, k: (g, m, 0)),
                  pl.BlockSpec((1, c), lambda g, m, k: (0, 0))],
        out_specs=pl.BlockSpec((bg, tm, c), lambda g, m, k: (g, m, 0)),
        out_shape=jax.ShapeDtypeStruct((bsz, length, c), MXU_DTYPE),
        scratch_shapes=[pltpu.VMEM((bg, tm, c), F32)],
        compiler_params=_cp(("parallel", "parallel", "arbitrary")),
        name="dft_inv",
    )(ginv, ginv, xspec, xspec, fspec, fspec, hb0, vx, x0, bias)


def _long_conv(vx, vxb, x0, filt, bias, mats, bsz, length):
    gfwd, ginv = mats
    c = vx.shape[-1]
    bg = math.gcd(bsz, 8)
    fspec = _dft_fwd(gfwd, filt.astype(MXU_DTYPE)[None], 1)[0]
    xspec = _dft_fwd(gfwd, vxb.reshape(bsz, length, c), bg)
    hb0 = filt[0:1, c:2 * c]
    out = _dft_inv(ginv, xspec, fspec, hb0, vx.reshape(bsz, length, c), x0.reshape(bsz, length, c),
                   bias, bg)
    return out.reshape(bsz * length, c)


def _rope_partner(r):
    a = ROPE_DIM // 2
    return a * (r // a) + (r % a + a // 2) % a


def _partner_lanes(a):
    idx = np.arange(a.shape[-1])
    r = idx % HEAD_PAD - NOPE_DIM
    rot = (r >= 0) & (r < ROPE_DIM)
    src = np.where(rot, idx - r + _rope_partner(np.clip(r, 0, ROPE_DIM - 1)), idx)
    return jnp.where(jnp.asarray(rot), a[..., src], jnp.zeros((), a.dtype))


def _head_norm_rope(x, xp, cg, sg, out_scale):
    ms = jnp.sum(x * x, axis=-1, keepdims=True) * (1.0 / QK_DIM)
    return (x * cg + xp * sg) * (lax.rsqrt(ms + EPS) * out_scale)


def _kv_kernel(kva_ref, kr_ref, kr2_ref, ng_ref, wk_ref, wv_ref, vone_ref, kg_ref, kg2_ref,
               cos_ref, sin_ref, ckv_ref, k_ref, v_ref, *, normalize):
    kva = kva_ref[...].astype(F32)
    ckv = _rms(kva, KV_RANK) * ng_ref[...] if normalize else kva
    if normalize:
        ckv_ref[...] = ckv
    cb = ckv.astype(MXU_DTYPE)
    kfull = jnp.dot(cb, wk_ref[...], preferred_element_type=F32)
    kr = kr_ref[...].astype(F32)
    cg = cos_ref[...] * kg_ref[...]
    sg = sin_ref[...] * kg2_ref[...]
    xp = kr2_ref[...].astype(F32)
    for h in range(N_HEADS):
        sl = slice(h * HEAD_PAD, (h + 1) * HEAD_PAD)
        k_ref[:, sl] = _head_norm_rope(kfull[:, sl] + kr, xp, cg, sg, 1.0).astype(k_ref.dtype)
    v = jnp.dot(cb, wv_ref[...], preferred_element_type=F32) + vone_ref[...]
    v_ref[...] = v.astype(v_ref.dtype)


def _kv_proj(src, kv_blk, kr_src, kr_blk, kr2_blk, norm_g, wk, wv, vone, kg, kg2, cos, sin,
             table_map, tl, normalize):
    nt = src.shape[0]
    const = lambda i: (0, 0)
    row = lambda i: (i, 0)
    hw = N_HEADS * HEAD_PAD
    out_shape = [jax.ShapeDtypeStruct((nt, KV_RANK), F32),
                 jax.ShapeDtypeStruct((nt, hw), MXU_DTYPE),
                 jax.ShapeDtypeStruct((nt, hw), MXU_DTYPE)]
    out_specs = [pl.BlockSpec((tl, KV_RANK), row),
                 pl.BlockSpec((tl, hw), row),
                 pl.BlockSpec((tl, hw), row)]
    n_in = 11
    kern = functools.partial(_kv_kernel, normalize=normalize)
    if not normalize:
        out_shape, out_specs = out_shape[1:], out_specs[1:]
        kern = lambda *refs: _kv_kernel(*refs[:n_in], None, *refs[n_in:], normalize=False)
    return pl.pallas_call(
        kern,
        grid=(nt // tl,),
        in_specs=[pl.BlockSpec((tl, KV_RANK), lambda i: (i, kv_blk)),
                  pl.BlockSpec((tl, HEAD_PAD), lambda i: (i, kr_blk)),
                  pl.BlockSpec((tl, HEAD_PAD), lambda i: (i, kr2_blk)),
                  pl.BlockSpec((1, KV_RANK), const),
                  pl.BlockSpec((KV_RANK, hw), const),
                  pl.BlockSpec((KV_RANK, hw), const),
                  pl.BlockSpec((1, hw), const),
                  pl.BlockSpec((1, HEAD_PAD), const),
                  pl.BlockSpec((1, HEAD_PAD), const),
                  pl.BlockSpec((tl, HEAD_PAD), table_map),
                  pl.BlockSpec((tl, HEAD_PAD), table_map)],
        out_specs=out_specs,
        out_shape=out_shape,
        compiler_params=_cp(("parallel",)),
        name="kv_proj" if normalize else "kv_proj_cache",
    )(src, kr_src, kr_src, norm_g, wk, wv, vone, kg, kg2, cos, sin)


def _q_kernel(qa_ref, ng_ref, wq_ref, wq2_ref, qg_ref, qg2_ref, cos_ref, sin_ref, q_ref):
    qn = (_rms(qa_ref[...].astype(F32), Q_RANK) * ng_ref[...]).astype(MXU_DTYPE)
    qfull = jnp.dot(qn, wq_ref[...], preferred_element_type=F32)
    qperm = jnp.dot(qn, wq2_ref[...], preferred_element_type=F32)
    cg = cos_ref[...] * qg_ref[...]
    sg = sin_ref[...] * qg2_ref[...]
    for h in range(N_HEADS):
        sl = slice(h * HEAD_PAD, (h + 1) * HEAD_PAD)
        q_ref[:, sl] = _head_norm_rope(qfull[:, sl], qperm[:, sl], cg, sg, SOFTMAX_C).astype(q_ref.dtype)


def _q_proj(proj, norm_g, wq, wq2, qg, qg2, cos, sin, table_map, tl):
    nt = proj.shape[0]
    const = lambda i: (0, 0)
    hw = N_HEADS * HEAD_PAD
    return pl.pallas_call(
        _q_kernel,
        grid=(nt // tl,),
        in_specs=[pl.BlockSpec((tl, Q_PAD), lambda i: (i, OFF_Q // Q_PAD)),
                  pl.BlockSpec((1, Q_PAD), const),
                  pl.BlockSpec((Q_PAD, hw), const),
                  pl.BlockSpec((Q_PAD, hw), const),
                  pl.BlockSpec((1, HEAD_PAD), const),
                  pl.BlockSpec((1, HEAD_PAD), const),
                  pl.BlockSpec((tl, HEAD_PAD), table_map),
                  pl.BlockSpec((tl, HEAD_PAD), table_map)],
        out_specs=pl.BlockSpec((tl, hw), lambda i: (i, 0)),
        out_shape=jax.ShapeDtypeStruct((nt, hw), MXU_DTYPE),
        compiler_params=_cp(("parallel",)),
        name="q_proj",
    )(proj, norm_g, wq, wq2, qg, qg2, cos, sin)


def _rope_tables(tl, dec_seq):
    a = ROPE_DIM // 2
    t = np.arange(dec_seq)
    pos = np.stack([t // GRID_W, t % GRID_W]).astype(np.float32)
    inv = np.power(np.float32(ROPE_BASE), -np.arange(0, a, 2, dtype=np.float32) / np.float32(a))
    ang = pos[:, :, None] * inv.astype(np.float32)
    cos = np.ones((tl + dec_seq, HEAD_PAD), np.float32)
    sin = np.zeros((tl + dec_seq, HEAD_PAD), np.float32)
    hq = a // 2
    for i in range(2):
        lo = NOPE_DIM + i * a
        cos[tl:, lo:lo + hq] = np.cos(ang[i])
        cos[tl:, lo + hq:lo + a] = np.cos(ang[i])
        sin[tl:, lo:lo + hq] = -np.sin(ang[i])
        sin[tl:, lo + hq:lo + a] = np.sin(ang[i])
    return jnp.asarray(cos), jnp.asarray(sin)


def _attn_kernel(*refs, has_ctx, chunk):
    if has_ctx:
        q_ref, k_ref, v_ref, kc_ref, vc_ref, o_ref = refs
    else:
        q_ref, k_ref, v_ref, o_ref = refs
    nt_dims = (((1,), (1,)), ((), ()))
    tq = q_ref.shape[0]
    length = k_ref.shape[0]
    chunks = [(k_ref, v_ref, j, min(chunk, length - j)) for j in range(0, length, chunk)]
    if has_ctx:
        chunks = [(kc_ref, vc_ref, 0, kc_ref.shape[0])] + chunks
    heads = [slice(h * HEAD_PAD, (h + 1) * HEAD_PAD) for h in range(q_ref.shape[1] // HEAD_PAD)]
    qs = [q_ref[:, hs] for hs in heads]
    m = [jnp.full((tq, 1), -jnp.inf, F32) for _ in heads]
    acc = [jnp.zeros((tq, HEAD_PAD), F32) for _ in heads]
    for kr, vr, start, size in chunks:
        rows = pl.ds(start, size)
        for h, hs in enumerate(heads):
            s = lax.dot_general(qs[h], kr[rows, hs], nt_dims, preferred_element_type=F32)
            m_new = jnp.maximum(m[h], jnp.max(s, axis=-1, keepdims=True))
            p = jnp.exp2(s - m_new).astype(MXU_DTYPE)
            acc[h] = acc[h] * jnp.exp2(m[h] - m_new) + jnp.dot(p, vr[rows, hs],
                                                               preferred_element_type=F32)
            m[h] = m_new
    outs = [a[:, 0:V_DIM] / a[:, V_DIM:V_DIM + 1] for a in acc]
    o_ref[...] = jnp.concatenate(outs, axis=-1).astype(o_ref.dtype)


def _attention(q, k, v, out_rows, row_off, bsz, length, tq, nh, ctx=None):
    assert row_off % length == 0 and length % tq == 0
    qb0 = row_off // tq
    kb0 = row_off // length
    nq = length // tq
    hp = N_HEADS // nh
    in_specs = [pl.BlockSpec((tq, nh * HEAD_PAD), lambda b, h, i: (qb0 + b * nq + i, h)),
                pl.BlockSpec((length, nh * HEAD_PAD), lambda b, h, i: (kb0 + b, h)),
                pl.BlockSpec((length, nh * HEAD_PAD), lambda b, h, i: (kb0 + b, h))]
    args = [q, k, v]
    if ctx is not None:
        kc, vc, past = ctx
        in_specs += [pl.BlockSpec((past, nh * HEAD_PAD), lambda b, h, i: (b, h)),
                     pl.BlockSpec((past, nh * HEAD_PAD), lambda b, h, i: (b, h))]
        args += [kc, vc]
    return pl.pallas_call(
        functools.partial(_attn_kernel, has_ctx=ctx is not None, chunk=min(length, 2048)),
        grid=(bsz, hp, nq),
        in_specs=in_specs,
        out_specs=pl.BlockSpec((tq, nh * V_DIM), lambda b, h, i: (b * nq + i, h)),
        out_shape=jax.ShapeDtypeStruct((out_rows, N_HEADS * V_DIM), MXU_DTYPE),
        compiler_params=_cp(("parallel", "parallel", "arbitrary")),
        name="attention_ctx" if ctx is None else "attention_latent",
    )(*args)


def _merge_kernel(pc_ref, pl_ref, hc_ref, hl_ref, ac_ref, al_ref, g0_ref, g1_ref, g2_ref, x_ref, mod_ref,
                  wp_ref, wh_ref, wa_ref, wo_ref, o_ref, *, n_ctx_tiles):
    is_ctx = pl.program_id(0) < n_ctx_tiles
    pick = lambda c_ref, l_ref: jnp.where(is_ctx, c_ref[...], l_ref[...])
    yp = jnp.dot(pick(pc_ref, pl_ref), wp_ref[...], preferred_element_type=F32)
    yh = jnp.dot(pick(hc_ref, hl_ref), wh_ref[...], preferred_element_type=F32)
    ya = jnp.dot(pick(ac_ref, al_ref), wa_ref[...], preferred_element_type=F32)
    gate = lambda ref: _sigmoid(ref[...].astype(F32))
    merged = gate(g0_ref) * yp + gate(g1_ref) * yh + gate(g2_ref) * ya
    mix = jnp.dot(merged.astype(MXU_DTYPE), wo_ref[...], preferred_element_type=F32)
    o_ref[...] = x_ref[...] + mod_ref[0][2:3] * mix


def _merge(p, hy, att, proj, x, mod, wp, wh, wa, wo, tm, ctx_rows, dec_seq):
    nt, d = x.shape
    row = lambda i: (i, 0)
    const = lambda i: (0, 0)
    g0 = OFF_G // d
    nc = ctx_rows // tm
    pair = lambda w: [pl.BlockSpec((tm, w), lambda i: (jnp.minimum(i, nc - 1), 0)),
                      pl.BlockSpec((tm, w), lambda i: (jnp.maximum(i - nc, 0), 0))]
    return pl.pallas_call(
        functools.partial(_merge_kernel, n_ctx_tiles=nc),
        grid=(nt // tm,),
        in_specs=pair(POOL_WIDTH) + pair(HY_WIDTH) + pair(N_HEADS * V_DIM) + [
                  pl.BlockSpec((tm, d), lambda i: (i, g0)),
                  pl.BlockSpec((tm, d), lambda i: (i, g0 + 1)),
                  pl.BlockSpec((tm, d), lambda i: (i, g0 + 2)),
                  pl.BlockSpec((tm, d), row),
                  pl.BlockSpec((1, 6, d), _mod_row_map(tm, ctx_rows, dec_seq)),
                  pl.BlockSpec((POOL_WIDTH, d), const),
                  pl.BlockSpec((HY_WIDTH, d), const),
                  pl.BlockSpec((N_HEADS * V_DIM, d), const),
                  pl.BlockSpec((d, d), const)],
        out_specs=pl.BlockSpec((tm, d), row),
        out_shape=jax.ShapeDtypeStruct((nt, d), F32),
        compiler_params=_cp(("parallel",)),
        name="merge",
    )(*p, *hy, *att, proj, proj, proj, x, mod, wp, wh, wa, wo)


def _norm2(x_ref, mod_ref, g_ref):
    m = mod_ref[0]
    return (_rms(x_ref[...], D_MODEL) * g_ref[...]) * (1.0 + m[4:5]) + m[3:4]


def _ffn_kernel(x_ref, mod_ref, g_ref, w1_ref, w3_ref, w2_ref, o_ref, *, sub):
    m = mod_ref[0]
    g = g_ref[...]
    for j in range(x_ref.shape[0] // sub):
        rs = pl.ds(j * sub, sub)
        x = x_ref[rs, :]
        h = ((_rms(x, D_MODEL) * g) * (1.0 + m[4:5]) + m[3:4]).astype(MXU_DTYPE)
        a = jnp.dot(h, w1_ref[...], preferred_element_type=F32)
        b = jnp.dot(h, w3_ref[...], preferred_element_type=F32)
        f = jnp.dot((_silu(a) * b).astype(MXU_DTYPE), w2_ref[...], preferred_element_type=F32)
        o_ref[rs, :] = x + m[5:6] * f


def _ffn_dense(x, mod, norm_g, w1, w3, w2, tm, ctx_rows, dec_seq):
    nt, d = x.shape
    ff = w1.shape[1]
    resident = lambda shape: pl.BlockSpec(shape, lambda i: (0, 0), pipeline_mode=pl.Buffered(1))
    return pl.pallas_call(
        functools.partial(_ffn_kernel, sub=min(tm, 256)),
        grid=(nt // tm,),
        in_specs=[pl.BlockSpec((tm, d), lambda i: (i, 0)),
                  pl.BlockSpec((1, 6, d), _mod_row_map(tm, ctx_rows, dec_seq)),
                  pl.BlockSpec((1, d), lambda i: (0, 0)),
                  resident((d, ff)), resident((d, ff)), resident((ff, d))],
        out_specs=pl.BlockSpec((tm, d), lambda i: (i, 0)),
        out_shape=jax.ShapeDtypeStruct((nt, d), F32),
        compiler_params=_cp(("parallel",)),
        name="ffn_dense",
    )(x, mod, norm_g, w1, w3, w2)


def _router_kernel(x_ref, mod_ref, g_ref, r_ref, h_ref, sel_ref):
    h = _norm2(x_ref, mod_ref, g_ref)
    h_ref[...] = h
    logits = _mm_f32(h, r_ref[...])
    lane = lax.broadcasted_iota(jnp.int32, logits.shape, 1).astype(F32)
    neg = jnp.float32(-jnp.inf)
    logits = jnp.where(lane < N_EXPERTS, logits, neg)
    m1 = jnp.max(logits, axis=-1, keepdims=True)
    i1 = jnp.min(jnp.where(logits == m1, lane, float(LANE)), axis=-1, keepdims=True)
    rest = jnp.where(lane == i1, neg, logits)
    m2 = jnp.max(rest, axis=-1, keepdims=True)
    i2 = jnp.min(jnp.where(rest == m2, lane, float(LANE)), axis=-1, keepdims=True)
    e2 = jnp.exp(m2 - m1)
    w_top = 1.0 / (1.0 + e2)
    sel_ref[...] = (jnp.where(lane == 0.0, i1, 0.0) + jnp.where(lane == 1.0, i2, 0.0)
                    + jnp.where(lane == 2.0, w_top, 0.0) + jnp.where(lane == 3.0, e2 * w_top, 0.0))


def _router(x, mod, norm_g, router, tm, ctx_rows, dec_seq):
    nt, d = x.shape
    rp = jnp.pad(router, ((0, 0), (0, LANE - N_EXPERTS)))
    return pl.pallas_call(
        _router_kernel,
        grid=(nt // tm,),
        in_specs=[pl.BlockSpec((tm, d), lambda i: (i, 0)),
                  pl.BlockSpec((1, 6, d), _mod_row_map(tm, ctx_rows, dec_seq)),
                  pl.BlockSpec((1, d), lambda i: (0, 0)),
                  pl.BlockSpec((d, LANE), lambda i: (0, 0))],
        out_specs=[pl.BlockSpec((tm, d), lambda i: (i, 0)),
                   pl.BlockSpec((tm, LANE), lambda i: (i, 0))],
        out_shape=[jax.ShapeDtypeStruct((nt, d), F32),
                   jax.ShapeDtypeStruct((nt, LANE), F32)],
        compiler_params=_cp(("parallel",)),
        name="router",
    )(x, mod, norm_g, rp)


def _dispatch_plan(sel, rows_per_tile):
    nt = sel.shape[0]
    n_pairs = 2 * nt
    r = rows_per_tile
    n_tiles = n_pairs // r + N_EXPERTS
    e_p = sel[:, 0:2].astype(jnp.int32).T.reshape(n_pairs)
    experts = jnp.arange(N_EXPERTS, dtype=jnp.int32)
    onehot = (e_p[:, None] == experts[None, :]).astype(jnp.int32)
    csum = jnp.cumsum(onehot, axis=0)
    rank = jnp.sum((csum - onehot) * onehot, axis=1)
    counts = csum[-1]
    padded = (counts + r - 1) // r * r
    ends = jnp.cumsum(padded)
    starts = ends - padded
    slot_p = jnp.sum(onehot * starts[None, :], axis=1) + rank
    tile_start = jnp.arange(n_tiles, dtype=jnp.int32) * r
    tile_expert = jnp.minimum(jnp.sum((tile_start[:, None] >= ends[None, :]).astype(jnp.int32), axis=1),
                              N_EXPERTS - 1)
    run_end = jnp.sum((tile_expert[:, None] == experts[None, :]) * (starts + counts)[None, :], axis=1)
    tile_rows = jnp.clip(run_end - tile_start, 0, r)
    return tile_expert, tile_rows, slot_p


def _row_copies(src, src_rows, dst, dst_rows, sem, n):
    return [pltpu.make_async_copy(src.at[pl.ds(src_rows(i), 1), :], dst.at[pl.ds(dst_rows(i), 1), :], sem)
            for i in range(n)]


def _start_burst(copies):
    for i, cp in enumerate(copies):
        cp.start(priority=i % 2)


def _dispatch_kernel(idx_ref, h_ref, xs_hbm, sem, *, n):
    _start_burst(_row_copies(h_ref, lambda i: i, xs_hbm, lambda i: idx_ref[0, 0, i], sem, n))
    pltpu.make_async_copy(h_ref, xs_hbm.at[pl.ds(0, n), :], sem).wait()


def _moe_dispatch(h, slot_p, n_slots, pb):
    nt, d = h.shape
    nb = nt // pb
    n_blocks = slot_p.shape[0] // pb
    return pl.pallas_call(
        functools.partial(_dispatch_kernel, n=pb),
        grid=(n_blocks,),
        in_specs=[pl.BlockSpec((1, 1, pb), lambda i: (i, 0, 0), memory_space=pltpu.SMEM),
                  pl.BlockSpec((pb, d), lambda i: (i % nb, 0))],
        out_specs=pl.BlockSpec(memory_space=pl.ANY),
        out_shape=jax.ShapeDtypeStruct((n_slots, d), F32),
        scratch_shapes=[pltpu.SemaphoreType.DMA(())],
        compiler_params=_cp(("arbitrary",)),
        name="moe_dispatch",
    )(slot_p.reshape(n_blocks, 1, pb), h)


def _moe_kernel(te_ref, tr_ref, xs_ref, w13_ref, w2_ref, ys_ref, *, rows, sub):
    n_rows = tr_ref[pl.program_id(0)]
    ff = w2_ref.shape[1]

    @pl.when(n_rows > 0)
    def _():
        for j in range(rows // sub):
            rs = pl.ds(j * sub, sub)
            occupied = j * sub + lax.broadcasted_iota(jnp.int32, (sub, 1), 0) < n_rows
            xb = jnp.where(occupied, xs_ref[rs, :], 0.0).astype(MXU_DTYPE)
            ab = jnp.dot(xb, w13_ref[0], preferred_element_type=F32)
            gated = (_silu(ab[:, 0:ff]) * ab[:, ff:2 * ff]).astype(MXU_DTYPE)
            ys_ref[rs, :] = jnp.dot(gated, w2_ref[0], preferred_element_type=F32)

    @pl.when(n_rows == 0)
    def _():
        ys_ref[...] = jnp.zeros_like(ys_ref)


def _ffn_moe(xs, tile_expert, tile_rows, w13, w2, rows):
    n_slots, d = xs.shape
    _, ff, _ = w2.shape
    wspec = lambda shape: pl.BlockSpec((1,) + shape, lambda t, te, tr: (te[t], 0, 0))
    grid_spec = pltpu.PrefetchScalarGridSpec(
        num_scalar_prefetch=2,
        grid=(n_slots // rows,),
        in_specs=[pl.BlockSpec((rows, d), lambda t, te, tr: (t, 0)),
                  wspec((d, 2 * ff)), wspec((ff, d))],
        out_specs=pl.BlockSpec((rows, d), lambda t, te, tr: (t, 0)))
    return pl.pallas_call(
        functools.partial(_moe_kernel, rows=rows, sub=min(rows, 256)),
        grid_spec=grid_spec,
        out_shape=jax.ShapeDtypeStruct((n_slots, d), F32),
        compiler_params=_cp(("arbitrary",)),
        name="ffn_moe",
    )(tile_expert, tile_rows, xs, w13, w2)


def _combine_kernel(idx0_ref, idx1_ref, x_ref, mod_ref, sel_ref, ys_hbm, o_ref, buf, sem, *, n):
    copies = (_row_copies(ys_hbm, lambda i: idx0_ref[0, 0, i], buf.at[0], lambda i: i, sem, n)
              + _row_copies(ys_hbm, lambda i: idx1_ref[0, 0, i], buf.at[1], lambda i: i, sem, n))
    _start_burst(copies)
    for k in range(2):
        pltpu.make_async_copy(ys_hbm.at[pl.ds(0, n), :], buf.at[k], sem).wait()
    sel = sel_ref[...]
    f = sel[:, 2:3] * buf[0] + sel[:, 3:4] * buf[1]
    o_ref[...] = x_ref[...] + mod_ref[0][5:6] * f


def _moe_combine(x, mod, sel, ys, slot_p, tm, ctx_rows, dec_seq, tile0=0, n_tiles=None):
    nt, d = x.shape
    nb = nt // tm
    n_tiles = nb if n_tiles is None else n_tiles
    idx = slot_p.reshape(2 * nb, 1, tm)
    row = lambda i: (i + tile0, 0)
    mod_map = _mod_row_map(tm, ctx_rows, dec_seq)
    return pl.pallas_call(
        functools.partial(_combine_kernel, n=tm),
        grid=(n_tiles,),
        in_specs=[pl.BlockSpec((1, 1, tm), lambda i: (i + tile0, 0, 0), memory_space=pltpu.SMEM),
                  pl.BlockSpec((1, 1, tm), lambda i: (nb + i + tile0, 0, 0), memory_space=pltpu.SMEM),
                  pl.BlockSpec((tm, d), row),
                  pl.BlockSpec((1, 6, d), lambda i: mod_map(i + tile0)),
                  pl.BlockSpec((tm, LANE), row),
                  pl.BlockSpec(memory_space=pl.ANY)],
        out_specs=pl.BlockSpec((tm, d), lambda i: (i, 0)),
        out_shape=jax.ShapeDtypeStruct((n_tiles * tm, d), F32),
        scratch_shapes=[pltpu.VMEM((2, tm, d), F32), pltpu.SemaphoreType.DMA(())],
        compiler_params=_cp(("arbitrary",)),
        name="moe_combine",
    )(idx, idx, x, mod, sel, ys)


def _prep_w_in(w_in):
    depth, d, _ = w_in.shape
    c1 = POOL_WIDTH
    c2 = c1 + 3 * HY_WIDTH
    c3 = c2 + Q_RANK
    c4 = c3 + KV_RANK
    c5 = c4 + ROPE_DIM
    z = lambda n: jnp.zeros((depth, d, n), w_in.dtype)
    kr_group = jnp.concatenate([z(NOPE_DIM), w_in[:, :, c4:c5], z(HEAD_PAD - QK_DIM)], axis=-1)
    cols = [w_in[:, :, c1:c2], w_in[:, :, 0:c1], w_in[:, :, c3:c4],
            kr_group, _partner_lanes(kr_group),
            w_in[:, :, c2:c3], z(Q_PAD - Q_RANK), w_in[:, :, c5:]]
    out = jnp.concatenate(cols, axis=-1).astype(MXU_DTYPE)
    assert out.shape[-1] == IN_PAD
    return out


def _head_pad_cols(w, width):
    lead = w.shape[:-1]
    w = w.reshape(*lead, N_HEADS, width)
    w = jnp.pad(w, [(0, 0)] * len(lead) + [(0, 0), (0, HEAD_PAD - width)])
    return w.reshape(*lead, N_HEADS * HEAD_PAD)


def _pad_last(a, n):
    return jnp.pad(a, [(0, 0)] * (a.ndim - 1) + [(0, n - a.shape[-1])])


def kernel(x_prompt, x_sample, cache_ckv, cache_krope, c, c_ctx, w_mod, b_mod, norm1_g, norm2_g, w_in, pool_w, pool_scale, pool_out, hy_conv_w, hy_conv_b, hy_w1, hy_b1, hy_w2, hy_b2, hy_w3, hy_freq, hy_decay, hy_bias, hy_out, q_norm_g, w_qb, kv_norm_g, w_kvb, qk_q_g, qk_k_g, mla_out, w_out, ffn_w1, ffn_w3, ffn_w2, moe_router, moe_w1, moe_w3, moe_w2):
    batch, seq, d = x_prompt.shape
    dec_batch, dec_seq, _ = x_sample.shape
    depth = w_mod.shape[0]
    past = cache_ckv.shape[2]
    ctx_rows = batch * seq
    lat_rows = dec_batch * dec_seq
    nt = ctx_rows + lat_rows
    tl = 256
    assert seq % tl == 0 and dec_seq % tl == 0 and ctx_rows % dec_seq == 0 and past % tl == 0
    tm = min(1024, math.gcd(ctx_rows, dec_seq))
    n_ctx_tiles = ctx_rows // tl
    bf = lambda a: a.astype(MXU_DTYPE)

    w_in_p = _prep_w_in(w_in)
    eye = jnp.eye(POOL_GROUPS, dtype=pool_w.dtype)
    gc = POOL_WIDTH // POOL_GROUPS
    pool_bd = bf(jnp.einsum('lgcd,gh->lgchd', pool_w, eye).reshape(depth, POOL_WIDTH, POOL_WIDTH))
    kvb = w_kvb.reshape(depth, KV_RANK, N_HEADS, NOPE_DIM + V_DIM)
    wk = bf(_head_pad_cols(kvb[..., :NOPE_DIM].reshape(depth, KV_RANK, N_HEADS * NOPE_DIM), NOPE_DIM))
    wv = bf(_head_pad_cols(kvb[..., NOPE_DIM:].reshape(depth, KV_RANK, N_HEADS * V_DIM), V_DIM))
    vone = jnp.tile((jnp.arange(HEAD_PAD) == V_DIM).astype(F32), N_HEADS)[None, :]
    wq_f = jnp.pad(_head_pad_cols(w_qb, QK_DIM), ((0, 0), (0, Q_PAD - Q_RANK), (0, 0)))
    wq, wq2 = bf(wq_f), bf(_partner_lanes(wq_f))
    q_norm_p = _pad_last(q_norm_g, Q_PAD)
    qg = _pad_last(qk_q_g, HEAD_PAD)
    kg = _pad_last(qk_k_g, HEAD_PAD)
    qg2, kg2 = _partner_lanes(qg), _partner_lanes(kg)
    pool_out_b, hy_out_b, mla_out_b, w_out_b = bf(pool_out), bf(hy_out), bf(mla_out), bf(w_out)
    ffn_w1_b, ffn_w3_b, ffn_w2_b = bf(ffn_w1), bf(ffn_w3), bf(ffn_w2)
    moe_w13_b, moe_w2_b = bf(jnp.concatenate([moe_w1, moe_w3], axis=-1)), bf(moe_w2)

    tp = math.gcd(512, math.gcd(ctx_rows, math.gcd(dec_seq, dec_batch * past)))
    cos_t, sin_t = _rope_tables(tp, dec_seq)
    n_ctx_tp = ctx_rows // tp
    n_lat_tp = dec_seq // tp
    table_map = lambda i: (jnp.where(i < n_ctx_tp, 0, 1 + (i - n_ctx_tp) % n_lat_tp), 0)
    ident_map = lambda i: (0, 0)
    mats_ctx = _dft_matrices(seq)
    mats_lat = _dft_matrices(dec_seq)

    n_cond = 1 + dec_batch
    cond_rows = -(-n_cond // SUBLANE) * SUBLANE
    cond = jnp.concatenate([c_ctx[None, :], c, jnp.zeros((cond_rows - n_cond, d), c.dtype)], axis=0)
    mod_all = _modulation(cond, w_mod, b_mod)[:, :n_cond].reshape(depth, n_cond, 6, d)
    filt_ctx = _hyena_filters(seq, hy_w1, hy_b1, hy_w2, hy_b2, hy_w3, hy_freq, hy_decay)
    filt_lat = _hyena_filters(dec_seq, hy_w1, hy_b1, hy_w2, hy_b2, hy_w3, hy_freq, hy_decay)
    cache_kr_p = jnp.pad(cache_krope, ((0, 0), (0, 0), (0, 0), (NOPE_DIM, HEAD_PAD - QK_DIM)))
    cache_kr_p = jnp.concatenate([cache_kr_p, _partner_lanes(cache_kr_p)], axis=-1)

    x = jnp.concatenate([x_prompt.reshape(ctx_rows, d), x_sample.reshape(lat_rows, d)], axis=0)
    ckv_list, kr_list = [], []
    for i in range(depth):
        j = i // 2
        mod = mod_all[i]
        r1 = lambda a: a[i].reshape(1, -1)
        proj = _inproj(x, mod, r1(norm1_g), w_in_p[i], tm, ctx_rows, dec_seq)

        local = lambda tile0, n: _local_mixers(proj, hy_conv_w[i], r1(hy_conv_b), pool_bd[i],
                                               r1(pool_scale), tl, tile0, n, n_ctx_tiles, seq, dec_seq)
        vx_c, vxb_c, x0_c, pooled_c = local(0, n_ctx_tiles)
        vx_l, vxb_l, x0_l, pooled_l = local(n_ctx_tiles, lat_rows // tl)
        bias = r1(hy_bias)
        hy_c = _long_conv(vx_c, vxb_c, x0_c, filt_ctx[i], bias, mats_ctx, batch, seq)
        hy_l = _long_conv(vx_l, vxb_l, x0_l, filt_lat[i], bias, mats_lat, dec_batch, dec_seq)

        ckv, k, v = _kv_proj(proj, OFF_KV // KV_RANK, proj, OFF_KROPE // HEAD_PAD, OFF_KROPE2 // HEAD_PAD,
                             r1(kv_norm_g), wk[i], wv[i], vone, r1(kg), r1(kg2), cos_t, sin_t,
                             table_map, tp, True)
        kc, vc = _kv_proj(cache_ckv[:, i].reshape(dec_batch * past, KV_RANK), 0,
                          cache_kr_p[:, i].reshape(dec_batch * past, 2 * HEAD_PAD), 0, 1,
                          r1(kv_norm_g), wk[i], wv[i], vone, r1(kg), r1(kg2), cos_t, sin_t,
                          ident_map, tp, False)
        q = _q_proj(proj, r1(q_norm_p), wq[i], wq2[i], r1(qg), r1(qg2), cos_t, sin_t, table_map, tp)
        att_c = _attention(q, k, v, ctx_rows, 0, batch, seq, min(seq, 256), 2)
        att_l = _attention(q, k, v, lat_rows, ctx_rows, dec_batch, dec_seq, min(dec_seq, 1024), 2,
                           ctx=(kc, vc, past))
        x = _merge((pooled_c, pooled_l), (hy_c, hy_l), (att_c, att_l), proj, x, mod,
                   pool_out_b[i], hy_out_b[i], mla_out_b[i], w_out_b[i], min(tm, 512), ctx_rows, dec_seq)

        if i % 2 == 1:
            h2, sel = _router(x, mod, r1(norm2_g), moe_router[j], tm, ctx_rows, dec_seq)
            tile_expert, tile_rows, slot_p = _dispatch_plan(sel, MOE_ROWS)
            xs = _moe_dispatch(h2, slot_p, tile_expert.shape[0] * MOE_ROWS, min(tm, 512))
            ys = _ffn_moe(xs, tile_expert, tile_rows, moe_w13_b[j], moe_w2_b[j], MOE_ROWS)
            tc = min(tm, 512)
            combine = functools.partial(_moe_combine, x, mod, sel, ys, slot_p, tc, ctx_rows, dec_seq)
            if i == depth - 1:
                x_parts = (combine(0, ctx_rows // tc), combine(ctx_rows // tc, lat_rows // tc))
            else:
                x = combine()
        else:
            x = _ffn_dense(x, mod, r1(norm2_g), ffn_w1_b[j], ffn_w3_b[j], ffn_w2_b[j],
                           tm, ctx_rows, dec_seq)

        ckv_list.append(ckv[:ctx_rows].reshape(batch, seq, KV_RANK))
        kr_list.append(proj[:ctx_rows, OFF_KROPE + NOPE_DIM:OFF_KROPE + QK_DIM].astype(F32)
                       .reshape(batch, seq, ROPE_DIM))

    if depth % 2 == 1:
        x_parts = (x[:ctx_rows], x[ctx_rows:])
    y_p = x_parts[0].reshape(batch, seq, d)
    y_s = x_parts[1].reshape(dec_batch, dec_seq, d)
    return (y_p, y_s, jnp.stack(ckv_list, axis=1), jnp.stack(kr_list, axis=1))
```
